```python
import math
import jax
import jax.numpy as jnp
from jax import lax
import numpy as np


D_MODEL = 4096
BATCH = 8
SEQ = 2048
DEPTH = 2

HEAD_DIM = 128
N_GROUPS = 4
GROUP_WIDTH = D_MODEL // N_GROUPS
N_HEADS = GROUP_WIDTH // HEAD_DIM
MIX_WIDTH = N_GROUPS * GROUP_WIDTH
Q_BLOCK = 128
EPS = 1e-6
NEG_INF = -1e30

DIFF_DH = HEAD_DIM // 2

DILATED_CFG = ((128, 1), (512, 4), (2048, 16))
DIL_BLOCK = 64

GRID_W = 64
NA_ROWS_MAX = 8
NA_COLS = 16

Q_LORA = 768
KV_LORA = 512
QK_NOPE = 128
QK_ROPE = 64
V_DIM = 128
ROPE_THETA = 10000.0

PEER_HEADS = 8
PEER_NKEYS = 128
PEER_EXPERTS = PEER_NKEYS * PEER_NKEYS
PEER_DQ = 256
PEER_TOPK = 16
PEER_CHUNK = 128

QKV_COLS = 3 * GROUP_WIDTH
MLA_COLS = Q_LORA + KV_LORA + QK_ROPE
IN_COLS = 3 * QKV_COLS + MLA_COLS

kernel_name = 'hybrid_parallel_mixer_peer_encoder'


def rms_norm(x, g):
    xf = x.astype(jnp.float32)
    y = xf * lax.rsqrt(jnp.mean(xf * xf, axis=-1, keepdims=True) + EPS)
    return (y * g.astype(jnp.float32)).astype(x.dtype)


def alibi_slopes(n):
    return jnp.asarray(2.0 ** (-8.0 * np.arange(1, n + 1) / n), dtype=jnp.float32)


def rope(x, cos, sin):
    x1, x2 = jnp.split(x, 2, axis=-1)
    c = cos.astype(x.dtype)
    s = sin.astype(x.dtype)
    return jnp.concatenate([x1 * c - x2 * s, x2 * c + x1 * s], axis=-1)


def block_attention(q, k, v, scale):
    B, S, H, _ = q.shape
    n_blk = S // Q_BLOCK
    q_blocks = q.reshape(B, n_blk, Q_BLOCK, H, q.shape[-1]).swapaxes(0, 1)

    def blk(q_blk):
        s = jnp.einsum('bqhd,bkhd->bhqk', q_blk, k, preferred_element_type=jnp.float32) * scale
        a = jax.nn.softmax(s, axis=-1)
        return jnp.einsum('bhqk,bkhd->bqhd', a.astype(v.dtype), v)

    o = lax.map(blk, q_blocks)
    return o.swapaxes(0, 1).reshape(B, S, H, v.shape[-1])


def diff_attention(p, lam_vecs, g_qk, g_out, lambda_init):
    B, S, _ = p.shape
    q = rms_norm(p[..., :GROUP_WIDTH].reshape(B, S, N_HEADS, 2, DIFF_DH), g_qk[0])
    k = rms_norm(p[..., GROUP_WIDTH:2 * GROUP_WIDTH].reshape(B, S, N_HEADS, 2, DIFF_DH), g_qk[1])
    v = p[..., 2 * GROUP_WIDTH:].reshape(B, S, N_HEADS, HEAD_DIM)
    lv = lam_vecs.astype(jnp.float32)
    lam = jnp.exp(jnp.sum(lv[0] * lv[1])) - jnp.exp(jnp.sum(lv[2] * lv[3])) + lambda_init
    slopes = alibi_slopes(N_HEADS)
    n_blk = S // Q_BLOCK
    q_blocks = q.reshape(B, n_blk, Q_BLOCK, N_HEADS, 2, DIFF_DH).swapaxes(0, 1)
    k_pos = jnp.arange(S)
    scale = DIFF_DH ** -0.5

    def block(args):
        q_blk, b = args
        s = jnp.einsum('bqhcd,bkhcd->bhcqk', q_blk, k, preferred_element_type=jnp.float32) * scale
        q_pos = b * Q_BLOCK + jnp.arange(Q_BLOCK)
        dist = jnp.abs(q_pos[:, None] - k_pos[None, :]).astype(jnp.float32)
        s = s - slopes[:, None, None, None] * dist
        a = jax.nn.softmax(s, axis=-1)
        w = a[:, :, 0] - lam * a[:, :, 1]
        return jnp.einsum('bhqk,bkhd->bqhd', w.astype(v.dtype), v)

    o = lax.map(block, (q_blocks, jnp.arange(n_blk)))
    o = o.swapaxes(0, 1).reshape(B, S, N_HEADS, HEAD_DIM)
    o = rms_norm(o, g_out) * (1.0 - lambda_init)
    return o.reshape(B, S, GROUP_WIDTH)


def dilated_branch(q, k, v, window, dil, slopes):
    B, S, H, dh = q.shape
    half = window // (2 * dil)
    L = S // dil
    n_blk = -(-L // DIL_BLOCK)
    Lp = n_blk * DIL_BLOCK
    span = DIL_BLOCK + 2 * half

    def to_sub(a):
        return a.reshape(B, L, dil, H, dh).transpose(0, 2, 3, 1, 4)

    qs = jnp.pad(to_sub(q), ((0, 0), (0, 0), (0, 0), (0, Lp - L), (0, 0)))
    kv_pad = ((0, 0), (0, 0), (0, 0), (half, Lp - L + half), (0, 0))
    ks = jnp.pad(to_sub(k), kv_pad)
    vs = jnp.pad(to_sub(v), kv_pad)
    k_idx = np.arange(n_blk)[:, None] * DIL_BLOCK + np.arange(span)[None, :]
    kb = ks[:, :, :, k_idx]
    vb = vs[:, :, :, k_idx]
    qb = qs.reshape(B, dil, H, n_blk, DIL_BLOCK, dh)
    s = jnp.einsum('brhnqd,brhnkd->brhnqk', qb, kb, preferred_element_type=jnp.float32) * dh ** -0.5
    q_l = np.arange(n_blk)[:, None] * DIL_BLOCK + np.arange(DIL_BLOCK)[None, :]
    k_l = k_idx - half
    rel = k_l[:, None, :] - q_l[:, :, None]
    valid = (np.abs(rel) <= half) & (k_l[:, None, :] >= 0) & (k_l[:, None, :] < L)
    dist = jnp.asarray(np.abs(rel) * dil, dtype=jnp.float32)
    s = s - slopes[:, None, None, None] * dist
    s = jnp.where(valid, s, NEG_INF)
    lse = jax.nn.logsumexp(s, axis=-1)
    a = jnp.exp(s - lse[..., None])
    o = jnp.einsum('brhnqk,brhnkd->brhnqd', a.astype(v.dtype), vb)
    o = o.reshape(B, dil, H, Lp, dh)[:, :, :, :L].transpose(0, 3, 1, 2, 4).reshape(B, S, H, dh)
    lse = lse.reshape(B, dil, H, Lp)[..., :L].transpose(0, 3, 1, 2).reshape(B, S, H)
    return o, lse


def dilated_attention(p, g_qk):
    B, S, _ = p.shape
    shp = (B, S, N_HEADS, HEAD_DIM)
    q = rms_norm(p[..., :GROUP_WIDTH].reshape(shp), g_qk[0])
    k = rms_norm(p[..., GROUP_WIDTH:2 * GROUP_WIDTH].reshape(shp), g_qk[1])
    v = p[..., 2 * GROUP_WIDTH:].reshape(shp)
    slopes = alibi_slopes(N_HEADS)
    outs, lses = [], []
    for window, dil in DILATED_CFG:
        o, lse = dilated_branch(q, k, v, window, dil, slopes)
        outs.append(o)
        lses.append(lse)
    wts = jax.nn.softmax(jnp.stack(lses, axis=0), axis=0)
    o = jnp.einsum('ibsh,ibshd->bshd', wts.astype(v.dtype), jnp.stack(outs, axis=0))
    return o.reshape(B, S, GROUP_WIDTH)


def neighbourhood_attention(p, g_qk, rpb):
    B, S, _ = p.shape
    rows = S // GRID_W
    kr = min(NA_ROWS_MAX, rows)
    shp = (B, rows, GRID_W, N_HEADS, HEAD_DIM)
    q = rms_norm(p[..., :GROUP_WIDTH].reshape(shp), g_qk[0])
    k = rms_norm(p[..., GROUP_WIDTH:2 * GROUP_WIDTH].reshape(shp), g_qk[1])
    v = p[..., 2 * GROUP_WIDTH:].reshape(shp)
    n_cb = GRID_W // NA_COLS
    span = 2 * NA_COLS
    q_col = np.arange(GRID_W).reshape(n_cb, NA_COLS)
    k_col = np.clip(np.arange(n_cb) * NA_COLS - NA_COLS // 2, 0, GRID_W - span)[:, None] + np.arange(span)[None, :]
    c_start = np.clip(q_col - NA_COLS // 2, 0, GRID_W - NA_COLS)
    col_ok = (k_col[:, None, :] >= c_start[:, :, None]) & (k_col[:, None, :] < c_start[:, :, None] + NA_COLS)
    col_idx = np.clip(k_col[:, None, :] - q_col[:, :, None] + NA_COLS - 1, 0, 2 * NA_COLS - 2)
    bias_col = rpb.astype(jnp.float32)[:, :, col_idx]
    scale = HEAD_DIM ** -0.5

    def row(args):
        q_row, r = args
        rs = jnp.clip(r - kr // 2, 0, rows - kr)
        kg = lax.dynamic_slice_in_dim(k, rs, kr, axis=1)[:, :, k_col]
        vg = lax.dynamic_slice_in_dim(v, rs, kr, axis=1)[:, :, k_col]
        qc = q_row.reshape(B, n_cb, NA_COLS, N_HEADS, HEAD_DIM)
        s = jnp.einsum('bcqhd,bjckhd->bhcqjk', qc, kg, preferred_element_type=jnp.float32) * scale
        row_idx = rs + jnp.arange(kr) - r + NA_ROWS_MAX - 1
        s = s + bias_col[:, row_idx].transpose(0, 2, 3, 1, 4)[None]
        s = jnp.where(col_ok[:, :, None, :], s, NEG_INF)
        a = jax.nn.softmax(s.reshape(B, N_HEADS, n_cb, NA_COLS, kr * span), axis=-1).reshape(s.shape)
        o = jnp.einsum('bhcqjk,bjckhd->bcqhd', a.astype(v.dtype), vg)
        return o.reshape(B, GRID_W, N_HEADS, HEAD_DIM)

    o = lax.map(row, (q.swapaxes(0, 1), jnp.arange(rows)))
    return o.swapaxes(0, 1).reshape(B, S, GROUP_WIDTH)


def mla_attention(p, g_cq, g_ckv, w_uq, w_ukv, g_qk, cos, sin):
    B, S, _ = p.shape
    c_q = rms_norm(p[..., :Q_LORA], g_cq)
    c_kv = rms_norm(p[..., Q_LORA:Q_LORA + KV_LORA], g_ckv)
    k_rope = p[..., Q_LORA + KV_LORA:]
    q = (c_q @ w_uq).reshape(B, S, N_HEADS, QK_NOPE + QK_ROPE)
    kv = (c_kv @ w_ukv).reshape(B, S, N_HEADS, QK_NOPE + V_DIM)
    k_nope, v = kv[..., :QK_NOPE], kv[..., QK_NOPE:]
    k = jnp.concatenate([k_nope, jnp.broadcast_to(k_rope[:, :, None, :], (B, S, N_HEADS, QK_ROPE))], axis=-1)
    q = rms_norm(q, g_qk[0])
    k = rms_norm(k, g_qk[1])
    q = jnp.concatenate([q[..., :QK_NOPE], rope(q[..., QK_NOPE:], cos, sin)], axis=-1)
    k = jnp.concatenate([k[..., :QK_NOPE], rope(k[..., QK_NOPE:], cos, sin)], axis=-1)
    o = block_attention(q, k, v, (QK_NOPE + QK_ROPE) ** -0.5)
    return o.reshape(B, S, N_HEADS * V_DIM)


def peer_ffn(h, w_q, sub_keys, u, v):
    B, S, D = h.shape
    T = B * S
    hf = h.reshape(T, D)
    q = (hf @ w_q).reshape(T, PEER_HEADS, 2, PEER_DQ // 2)
    sc = jnp.einsum('thcd,hcnd->thcn', q, sub_keys, preferred_element_type=jnp.float32)
    top_s, top_i = lax.top_k(sc, PEER_TOPK)
    cand_s = (top_s[:, :, 0, :, None] + top_s[:, :, 1, None, :]).reshape(T, PEER_HEADS, PEER_TOPK * PEER_TOPK)
    cand_i = (top_i[:, :, 0, :, None] * PEER_NKEYS + top_i[:, :, 1, None, :]).reshape(T, PEER_HEADS, PEER_TOPK * PEER_TOPK)
    best_s, best_j = lax.top_k(cand_s, PEER_TOPK)
    expert = jnp.take_along_axis(cand_i, best_j, axis=-1)
    gate = jax.nn.softmax(best_s, axis=-1)
    n_chunk = T // PEER_CHUNK
    n_sel = PEER_HEADS * PEER_TOPK

    def chunk(args):
        h_c, e_c, g_c = args
        act = jax.nn.gelu(jnp.einsum('cd,ced->ce', h_c, u[e_c]), approximate=False)
        return jnp.einsum('ce,ced->cd', (g_c * act).astype(h_c.dtype), v[e_c])

    out = lax.map(chunk, (hf.reshape(n_chunk, PEER_CHUNK, D),
                          expert.reshape(n_chunk, PEER_CHUNK, n_sel),
                          gate.reshape(n_chunk, PEER_CHUNK, n_sel)))
    return out.reshape(B, S, D)


def setup_inputs(seed: int = 0) -> dict:
    key = jax.random.key(seed)
    ks = jax.random.split(key, 20)

    def nrm(k, shape, scale):
        return jax.random.normal(k, shape, jnp.float32) * scale

    def gain(k, shape):
        return 1.0 + 0.01 * jax.random.normal(k, shape, jnp.float32)

    L = DEPTH
    return {
        'x': nrm(ks[0], (BATCH, SEQ, D_MODEL), 1.0),
        'g_mix': gain(ks[1], (L, D_MODEL)),
        'w_in': nrm(ks[2], (L, D_MODEL, IN_COLS), D_MODEL ** -0.5),
        'w_out': nrm(ks[3], (L, MIX_WIDTH, D_MODEL), MIX_WIDTH ** -0.5),
        'diff_lambda': nrm(ks[4], (L, 4, DIFF_DH), 0.1),
        'diff_g_qk': gain(ks[5], (L, 2, DIFF_DH)),
        'diff_g_out': gain(ks[6], (L, HEAD_DIM)),
        'dil_g_qk': gain(ks[7], (L, 2, HEAD_DIM)),
        'na_g_qk': gain(ks[8], (L, 2, HEAD_DIM)),
        'na_rpb': nrm(ks[9], (L, N_HEADS, 2 * NA_ROWS_MAX - 1, 2 * NA_COLS - 1), 0.1),
        'mla_g_cq': gain(ks[10], (L, Q_LORA)),
        'mla_g_ckv': gain(ks[11], (L, KV_LORA)),
        'mla_w_uq': nrm(ks[12], (L, Q_LORA, N_HEADS * (QK_NOPE + QK_ROPE)), Q_LORA ** -0.5),
        'mla_w_ukv': nrm(ks[13], (L, KV_LORA, N_HEADS * (QK_NOPE + V_DIM)), KV_LORA ** -0.5),
        'mla_g_qk': gain(ks[14], (L, 2, QK_NOPE + QK_ROPE)),
        'g_ffn': gain(ks[15], (L, D_MODEL)),
        'peer_w_q': nrm(ks[16], (L, D_MODEL, PEER_HEADS * PEER_DQ), D_MODEL ** -0.5),
        'peer_sub_keys': nrm(ks[17], (L, PEER_HEADS, 2, PEER_NKEYS, PEER_DQ // 2), (PEER_DQ // 2) ** -0.5),
        'peer_u': nrm(ks[18], (L, PEER_EXPERTS, D_MODEL), D_MODEL ** -0.5),
        'peer_v': nrm(ks[19], (L, PEER_EXPERTS, D_MODEL), PEER_HEADS ** -0.5),
    }


def reference(x, g_mix, w_in, w_out, diff_lambda, diff_g_qk, diff_g_out, dil_g_qk, na_g_qk, na_rpb,
              mla_g_cq, mla_g_ckv, mla_w_uq, mla_w_ukv, mla_g_qk, g_ffn, peer_w_q, peer_sub_keys,
              peer_u, peer_v):
    B, S, D = x.shape
    pos = jnp.arange(S, dtype=jnp.float32)
    inv_freq = 1.0 / (ROPE_THETA ** (jnp.arange(0, QK_ROPE, 2, dtype=jnp.float32) / QK_ROPE))
    ang = pos[:, None] * inv_freq[None, :]
    cos = jnp.cos(ang)[:, None, :]
    sin = jnp.sin(ang)[:, None, :]
    for l in range(DEPTH):
        lambda_init = 0.8 - 0.6 * math.exp(-0.3 * l)
        h = rms_norm(x, g_mix[l])
        p = h @ w_in[l]
        o_a = diff_attention(p[..., :QKV_COLS], diff_lambda[l], diff_g_qk[l], diff_g_out[l], lambda_init)
        o_b = dilated_attention(p[..., QKV_COLS:2 * QKV_COLS], dil_g_qk[l])
        o_c = neighbourhood_attention(p[..., 2 * QKV_COLS:3 * QKV_COLS], na_g_qk[l], na_rpb[l])
        o_d = mla_attention(p[..., 3 * QKV_COLS:], mla_g_cq[l], mla_g_ckv[l], mla_w_uq[l], mla_w_ukv[l],
                            mla_g_qk[l], cos, sin)
        x = x + jnp.concatenate([o_a, o_b, o_c, o_d], axis=-1) @ w_out[l]
        x = x + peer_ffn(rms_norm(x, g_ffn[l]), peer_w_q[l], peer_sub_keys[l], peer_u[l], peer_v[l])
    return x
```

```python
import functools
import math

import jax
import jax.numpy as jnp
import numpy as np
from jax import lax
from jax.experimental import pallas as pl
from jax.experimental.pallas import tpu as pltpu

F32 = jnp.float32
BF16 = jnp.bfloat16
I32 = jnp.int32

D_MODEL = 4096
HEAD_DIM = 128
N_HEADS = 8
GROUP_WIDTH = N_HEADS * HEAD_DIM
QKV_COLS = 3 * GROUP_WIDTH
EPS = 1e-6
NEG_INF = -1e30
DIFF_DH = HEAD_DIM // 2
DIL_HALF = 64
DIL_DILATIONS = (1, 4, 16)
GRID_W = 64
NA_ROWS = 8
NA_COLS = 16
NA_Q_ROWS = 4
NA_K_ROWS = NA_Q_ROWS + NA_ROWS
Q_LORA = 768
KV_LORA = 512
QK_NOPE = 128
QK_ROPE = 64
MLA_QK = QK_NOPE + QK_ROPE
MLA_QK_PAD = 2 * HEAD_DIM
ROPE_THETA = 10000.0
PEER_HEADS = 8
PEER_NKEYS = 128
PEER_TOPK = 16
PEER_EXPERTS = PEER_NKEYS * PEER_NKEYS
MLA_LATENT_PAD = Q_LORA + KV_LORA + HEAD_DIM

VMEM_LIMIT = 48 * 1024 * 1024

_NT = (((1,), (1,)), ((), ()))


def _params(sem):
    return pltpu.CompilerParams(dimension_semantics=sem, vmem_limit_bytes=VMEM_LIMIT)


def _rms_rows_kernel(x_ref, g_ref, o_ref):
    x = x_ref[...]
    ms = jnp.mean(x * x, axis=-1, keepdims=True)
    o_ref[...] = (x * lax.rsqrt(ms + EPS) * g_ref[...]).astype(o_ref.dtype)


def rms_rows(x, g, tm=256):
    t, c = x.shape
    return pl.pallas_call(
        _rms_rows_kernel,
        grid=(t // tm,),
        in_specs=[pl.BlockSpec((tm, c), lambda i: (i, 0)), pl.BlockSpec((1, c), lambda i: (0, 0))],
        out_specs=pl.BlockSpec((tm, c), lambda i: (i, 0)),
        out_shape=jax.ShapeDtypeStruct((t, c), BF16),
        compiler_params=_params(("parallel",)),
        name="rms_rows",
    )(x, g.reshape(1, c))


def _mla_latent_kernel(p_ref, gq_ref, gkv_ref, cq_ref, ckv_ref, kr_ref):
    cq = p_ref[:, :Q_LORA]
    ckv = p_ref[:, Q_LORA:Q_LORA + KV_LORA]
    cq_ref[...] = (cq * lax.rsqrt(jnp.mean(cq * cq, axis=-1, keepdims=True) + EPS) * gq_ref[...]).astype(BF16)
    ckv_ref[...] = (ckv * lax.rsqrt(jnp.mean(ckv * ckv, axis=-1, keepdims=True) + EPS) * gkv_ref[...]).astype(BF16)
    kr_ref[...] = p_ref[:, Q_LORA + KV_LORA:]


def mla_latent_norm(p_lat, g_cq, g_ckv, tm=512):
    t = p_lat.shape[0]
    return pl.pallas_call(
        _mla_latent_kernel,
        grid=(t // tm,),
        in_specs=[pl.BlockSpec((tm, MLA_LATENT_PAD), lambda i: (i, 0)),
                  pl.BlockSpec((1, Q_LORA), lambda i: (0, 0)),
                  pl.BlockSpec((1, KV_LORA), lambda i: (0, 0))],
        out_specs=[pl.BlockSpec((tm, Q_LORA), lambda i: (i, 0)),
                   pl.BlockSpec((tm, KV_LORA), lambda i: (i, 0)),
                   pl.BlockSpec((tm, HEAD_DIM), lambda i: (i, 0))],
        out_shape=[jax.ShapeDtypeStruct((t, Q_LORA), BF16),
                   jax.ShapeDtypeStruct((t, KV_LORA), BF16),
                   jax.ShapeDtypeStruct((t, HEAD_DIM), F32)],
        compiler_params=_params(("parallel",)),
        name="mla_latent_norm",
    )(p_lat, g_cq.reshape(1, Q_LORA), g_ckv.reshape(1, KV_LORA))


def _mm_kernel(a_ref, b_ref, o_ref, acc_ref, *, nk):
    k = pl.program_id(2)

    @pl.when(k == 0)
    def _():
        acc_ref[...] = jnp.zeros_like(acc_ref)

    acc_ref[...] += jnp.dot(a_ref[...], b_ref[...], preferred_element_type=F32)

    @pl.when(k == nk - 1)
    def _():
        o_ref[...] = acc_ref[...].astype(o_ref.dtype)


def _mm_res_kernel(a_ref, b_ref, r_ref, o_ref, acc_ref, *, nk):
    k = pl.program_id(2)

    @pl.when(k == 0)
    def _():
        acc_ref[...] = jnp.zeros_like(acc_ref)

    acc_ref[...] += jnp.dot(a_ref[...], b_ref[...], preferred_element_type=F32)

    @pl.when(k == nk - 1)
    def _():
        o_ref[...] = r_ref[...] + acc_ref[...]


def matmul(a, b, *, out_dtype=F32, residual=None, tm=1024, tn=1024, tk=512):
    m, kd = a.shape
    n = b.shape[1]
    tm, tn, tk = min(tm, m), min(tn, n), min(tk, kd)
    nk = kd // tk
    in_specs = [pl.BlockSpec((tm, tk), lambda i, j, k: (i, k)),
                pl.BlockSpec((tk, tn), lambda i, j, k: (k, j))]
    args = [a, b]
    if residual is None:
        body = functools.partial(_mm_kernel, nk=nk)
    else:
        body = functools.partial(_mm_res_kernel, nk=nk)
        in_specs.append(pl.BlockSpec((tm, tn), lambda i, j, k: (i, j)))
        args.append(residual)
    return pl.pallas_call(
        body,
        grid=(m // tm, n // tn, nk),
        in_specs=in_specs,
        out_specs=pl.BlockSpec((tm, tn), lambda i, j, k: (i, j)),
        out_shape=jax.ShapeDtypeStruct((m, n), out_dtype),
        scratch_shapes=[pltpu.VMEM((tm, tn), F32)],
        compiler_params=_params(("parallel", "parallel", "arbitrary")),
        name="matmul_res" if residual is not None else "matmul",
    )(*args)


def _gelu_exact(x):
    return 0.5 * x * (1.0 + lax.erf(x * (1.0 / math.sqrt(2.0))))


def _peer_act_kernel(h_ref, u_ref, g_ref, o_ref, acc_ref, *, nk):
    k = pl.program_id(2)

    @pl.when(k == 0)
    def _():
        acc_ref[...] = jnp.zeros_like(acc_ref)

    acc_ref[...] += lax.dot_general(h_ref[...], u_ref[...], _NT, preferred_element_type=F32)

    @pl.when(k == nk - 1)
    def _():
        o_ref[...] = (_gelu_exact(acc_ref[...]) * g_ref[...].astype(F32)).astype(o_ref.dtype)


def peer_gated_act(hn, u, gates, tm=1024, tn=1024, tk=512):
    t, d = hn.shape
    e = u.shape[0]
    tm = min(tm, t)
    nk = d // tk
    return pl.pallas_call(
        functools.partial(_peer_act_kernel, nk=nk),
        grid=(t // tm, e // tn, nk),
        in_specs=[pl.BlockSpec((tm, tk), lambda i, j, k: (i, k)),
                  pl.BlockSpec((tn, tk), lambda i, j, k: (j, k)),
                  pl.BlockSpec((tm, tn), lambda i, j, k: (i, j))],
        out_specs=pl.BlockSpec((tm, tn), lambda i, j, k: (i, j)),
        out_shape=jax.ShapeDtypeStruct((t, e), BF16),
        scratch_shapes=[pltpu.VMEM((tm, tn), F32)],
        compiler_params=_params(("parallel", "parallel", "arbitrary")),
        name="peer_gated_act",
    )(hn, u, gates)


def _rms_lanes(x, g):
    ms = jnp.mean(x * x, axis=-1, keepdims=True)
    return x * lax.rsqrt(ms + EPS) * g


def _rms_halves(x, g):
    lane = lax.broadcasted_iota(I32, x.shape, 1)
    lo = lane < DIFF_DH
    xx = x * x
    ms_lo = jnp.sum(jnp.where(lo, xx, 0.0), axis=-1, keepdims=True) * (1.0 / DIFF_DH)
    ms_hi = jnp.sum(jnp.where(lo, 0.0, xx), axis=-1, keepdims=True) * (1.0 / DIFF_DH)
    inv = jnp.where(lo, lax.rsqrt(ms_lo + EPS), lax.rsqrt(ms_hi + EPS))
    return x * inv * g


def _rel_pos(tq, s, q0):
    return lax.broadcasted_iota(I32, (tq, s), 1) - lax.broadcasted_iota(I32, (tq, s), 0) - q0


def _diff_attn_kernel(slope_ref, q_ref, k_ref, v_ref, gq_ref, gk_ref, go_ref, lam_ref, o_ref, kn_ref, vb_ref,
                      *, tq, lambda_init):
    h = pl.program_id(1)
    qi = pl.program_id(2)
    s_len = k_ref.shape[0]

    @pl.when(qi == 0)
    def _():
        kn_ref[...] = _rms_halves(k_ref[...], gk_ref[...]).astype(BF16)
        vb_ref[...] = v_ref[...].astype(BF16)

    lv = lam_ref[...]
    lam = (jnp.exp(jnp.sum(lv[0:1] * lv[1:2], axis=-1, keepdims=True))
           - jnp.exp(jnp.sum(lv[2:3] * lv[3:4], axis=-1, keepdims=True)) + lambda_init)

    qn = _rms_halves(q_ref[...], gq_ref[...]) * (DIFF_DH ** -0.5)
    lane = lax.broadcasted_iota(I32, qn.shape, 1)
    q0 = jnp.where(lane < DIFF_DH, qn, 0.0).astype(BF16)
    q1 = jnp.where(lane < DIFF_DH, 0.0, qn).astype(BF16)
    kn = kn_ref[...]
    bias = slope_ref[h] * jnp.abs(_rel_pos(tq, s_len, qi * tq)).astype(F32)

    def softmax_parts(qc):
        s = lax.dot_general(qc, kn, _NT, preferred_element_type=F32) - bias
        e = jnp.exp(s - jnp.max(s, axis=-1, keepdims=True))
        return e, jnp.sum(e, axis=-1, keepdims=True)

    e0, l0 = softmax_parts(q0)
    e1, l1 = softmax_parts(q1)
    w = e0 * (1.0 / l0) - e1 * (lam / l1)
    o = jnp.dot(w.astype(BF16), vb_ref[...], preferred_element_type=F32)
    o_ref[...] = (_rms_lanes(o, go_ref[...]) * (1.0 - lambda_init)).astype(o_ref.dtype)


def diff_attention(p, col0, lam_vecs, g_qk, g_out, lambda_init, tq=256):
    b, s, _ = p.shape
    slopes = jnp.asarray(2.0 ** (-8.0 * np.arange(1, N_HEADS + 1) / N_HEADS), F32)
    lam_pad = jnp.pad(lam_vecs, ((0, 0), (0, HEAD_DIM - DIFF_DH)))
    gq = jnp.tile(g_qk[0], 2).reshape(1, HEAD_DIM)
    gk = jnp.tile(g_qk[1], 2).reshape(1, HEAD_DIM)
    vec = pl.BlockSpec((1, HEAD_DIM), lambda bi, h, qi: (0, 0))
    return pl.pallas_call(
        functools.partial(_diff_attn_kernel, tq=tq, lambda_init=lambda_init),
        grid=(b, N_HEADS, s // tq),
        in_specs=[pl.BlockSpec(memory_space=pltpu.SMEM),
                  pl.BlockSpec((None, tq, HEAD_DIM), lambda bi, h, qi: (bi, qi, col0 + h)),
                  pl.BlockSpec((None, s, HEAD_DIM), lambda bi, h, qi: (bi, 0, col0 + N_HEADS + h)),
                  pl.BlockSpec((None, s, HEAD_DIM), lambda bi, h, qi: (bi, 0, col0 + 2 * N_HEADS + h)),
                  vec, vec, vec,
                  pl.BlockSpec((4, HEAD_DIM), lambda bi, h, qi: (0, 0))],
        out_specs=pl.BlockSpec((None, tq, HEAD_DIM), lambda bi, h, qi: (bi, qi, h)),
        out_shape=jax.ShapeDtypeStruct((b, s, GROUP_WIDTH), BF16),
        scratch_shapes=[pltpu.VMEM((s, HEAD_DIM), BF16), pltpu.VMEM((s, HEAD_DIM), BF16)],
        compiler_params=_params(("parallel", "parallel", "arbitrary")),
        name="diff_attention",
    )(slopes, p, p, p, gq, gk, g_out.reshape(1, HEAD_DIM), lam_pad)


def _dilated_attn_kernel(slope_ref, q_ref, k_ref, v_ref, gq_ref, gk_ref, o_ref, kn_ref, vb_ref, *, tq):
    h = pl.program_id(1)
    qi = pl.program_id(2)
    s_len = k_ref.shape[0]

    @pl.when(qi == 0)
    def _():
        kn_ref[...] = _rms_lanes(k_ref[...], gk_ref[...]).astype(BF16)
        vb_ref[...] = v_ref[...].astype(BF16)

    qn = (_rms_lanes(q_ref[...], gq_ref[...]) * (HEAD_DIM ** -0.5)).astype(BF16)
    d = _rel_pos(tq, s_len, qi * tq)
    ad = jnp.abs(d)
    mult = jnp.zeros((tq, s_len), F32)
    for dil in DIL_DILATIONS:
        mult = mult + jnp.where(((d & (dil - 1)) == 0) & (ad <= DIL_HALF * dil), 1.0, 0.0)
    s = lax.dot_general(qn, kn_ref[...], _NT, preferred_element_type=F32) - slope_ref[h] * ad.astype(F32)
    s = jnp.where(mult > 0.0, s, NEG_INF)
    e = jnp.exp(s - jnp.max(s, axis=-1, keepdims=True)) * mult
    l = jnp.sum(e, axis=-1, keepdims=True)
    o = jnp.dot(e.astype(BF16), vb_ref[...], preferred_element_type=F32)
    o_ref[...] = (o * (1.0 / l)).astype(o_ref.dtype)


def dilated_attention(p, col0, g_qk, tq=256):
    b, s, _ = p.shape
    slopes = jnp.asarray(2.0 ** (-8.0 * np.arange(1, N_HEADS + 1) / N_HEADS), F32)
    vec = pl.BlockSpec((1, HEAD_DIM), lambda bi, h, qi: (0, 0))
    return pl.pallas_call(
        functools.partial(_dilated_attn_kernel, tq=tq),
        grid=(b, N_HEADS, s // tq),
        in_specs=[pl.BlockSpec(memory_space=pltpu.SMEM),
                  pl.BlockSpec((None, tq, HEAD_DIM), lambda bi, h, qi: (bi, qi, col0 + h)),
                  pl.BlockSpec((None, s, HEAD_DIM), lambda bi, h, qi: (bi, 0, col0 + N_HEADS + h)),
                  pl.BlockSpec((None, s, HEAD_DIM), lambda bi, h, qi: (bi, 0, col0 + 2 * N_HEADS + h)),
                  vec, vec],
        out_specs=pl.BlockSpec((None, tq, HEAD_DIM), lambda bi, h, qi: (bi, qi, h)),
        out_shape=jax.ShapeDtypeStruct((b, s, GROUP_WIDTH), BF16),
        scratch_shapes=[pltpu.VMEM((s, HEAD_DIM), BF16), pltpu.VMEM((s, HEAD_DIM), BF16)],
        compiler_params=_params(("parallel", "parallel", "arbitrary")),
        name="dilated_attention",
    )(slopes, p, p, p, g_qk[0].reshape(1, HEAD_DIM), g_qk[1].reshape(1, HEAD_DIM))


def _na_window_start(q_row0, n_rows):
    return jnp.clip(q_row0 - NA_ROWS // 2, 0, n_rows - NA_K_ROWS)


def _na_attn_kernel(q_ref, k_ref, v_ref, gq_ref, gk_ref, bias_ref, o_ref, kn_ref, vb_ref):
    qi = pl.program_id(2)
    n_rows = k_ref.shape[0] // GRID_W

    @pl.when(qi == 0)
    def _():
        kn_ref[...] = _rms_lanes(k_ref[...], gk_ref[...]).astype(BF16)
        vb_ref[...] = v_ref[...].astype(BF16)

    k0 = pl.multiple_of(_na_window_start(qi * NA_Q_ROWS, n_rows) * GRID_W, GRID_W)
    kw = kn_ref[pl.ds(k0, NA_K_ROWS * GRID_W), :]
    vw = vb_ref[pl.ds(k0, NA_K_ROWS * GRID_W), :]
    qn = (_rms_lanes(q_ref[...], gq_ref[...]) * (HEAD_DIM ** -0.5)).astype(BF16)
    s = lax.dot_general(qn, kw, _NT, preferred_element_type=F32) + bias_ref[...]
    e = jnp.exp(s - jnp.max(s, axis=-1, keepdims=True))
    l = jnp.sum(e, axis=-1, keepdims=True)
    o = jnp.dot(e.astype(BF16), vw, preferred_element_type=F32)
    o_ref[...] = (o * (1.0 / l)).astype(o_ref.dtype)


def _na_bias_tables(n_rows):
    nqb = n_rows // NA_Q_ROWS
    r0 = np.arange(nqb) * NA_Q_ROWS
    ws = np.clip(r0 - NA_ROWS // 2, 0, n_rows - NA_K_ROWS)
    qi = np.arange(NA_Q_ROWS * GRID_W)
    kj = np.arange(NA_K_ROWS * GRID_W)
    rq = r0[:, None] + (qi // GRID_W)[None, :]
    rk = ws[:, None] + (kj // GRID_W)[None, :]
    cq, ck = qi % GRID_W, kj % GRID_W
    rs = np.clip(rq - NA_ROWS // 2, 0, n_rows - NA_ROWS)
    row_ok = (rk[:, None, :] >= rs[:, :, None]) & (rk[:, None, :] < rs[:, :, None] + NA_ROWS)
    row_idx = np.clip(rk[:, None, :] - rq[:, :, None] + NA_ROWS - 1, 0, 2 * NA_ROWS - 2)
    cs = np.clip(cq - NA_COLS // 2, 0, GRID_W - NA_COLS)
    col_ok = (ck[None, :] >= cs[:, None]) & (ck[None, :] < cs[:, None] + NA_COLS)
    col_idx = np.clip(ck[None, :] - cq[:, None] + NA_COLS - 1, 0, 2 * NA_COLS - 2)
    valid = row_ok & col_ok[None]
    flat = row_idx * (2 * NA_COLS - 1) + col_idx[None]
    return flat.astype(np.int32), valid


def neighbourhood_attention(p, col0, g_qk, rpb):
    b, s, _ = p.shape
    n_rows = s // GRID_W
    tq = NA_Q_ROWS * GRID_W
    tk = NA_K_ROWS * GRID_W
    flat, valid = _na_bias_tables(n_rows)
    bias = jnp.where(valid[None], jnp.take(rpb.reshape(N_HEADS, -1), flat, axis=1), NEG_INF)
    vec = pl.BlockSpec((1, HEAD_DIM), lambda bi, h, qi: (0, 0))
    return pl.pallas_call(
        _na_attn_kernel,
        grid=(b, N_HEADS, s // tq),
        in_specs=[pl.BlockSpec((None, tq, HEAD_DIM), lambda bi, h, qi: (bi, qi, col0 + h)),
                  pl.BlockSpec((None, s, HEAD_DIM), lambda bi, h, qi: (bi, 0, col0 + N_HEADS + h)),
                  pl.BlockSpec((None, s, HEAD_DIM), lambda bi, h, qi: (bi, 0, col0 + 2 * N_HEADS + h)),
                  vec, vec,
                  pl.BlockSpec((None, None, tq, tk), lambda bi, h, qi: (h, qi, 0, 0))],
        out_specs=pl.BlockSpec((None, tq, HEAD_DIM), lambda bi, h, qi: (bi, qi, h)),
        out_shape=jax.ShapeDtypeStruct((b, s, GROUP_WIDTH), BF16),
        scratch_shapes=[pltpu.VMEM((s, HEAD_DIM), BF16), pltpu.VMEM((s, HEAD_DIM), BF16)],
        compiler_params=_params(("parallel", "parallel", "arbitrary")),
        name="neighbourhood_attention",
    )(p, p, p, g_qk[0].reshape(1, HEAD_DIM), g_qk[1].reshape(1, HEAD_DIM), bias)


def _mla_norm_rope(x, g, cos, sin):
    ms = jnp.sum(x * x, axis=-1, keepdims=True) * (1.0 / MLA_QK)
    xn = x * lax.rsqrt(ms + EPS) * g
    xr = xn[:, HEAD_DIM:]
    partner = pltpu.roll(xr, QK_ROPE // 2, 1) + pltpu.roll(xr, HEAD_DIM - QK_ROPE // 2, 1)
    return jnp.concatenate([xn[:, :HEAD_DIM], xr * cos + partner * sin], axis=-1)


def _mla_attn_kernel(q_ref, kn_in_ref, v_ref, kr_ref, cosq_ref, sinq_ref, cosk_ref, sink_ref, gq_ref, gk_ref,
                     o_ref, kf_ref, vb_ref):
    qi = pl.program_id(2)

    @pl.when(qi == 0)
    def _():
        k = jnp.concatenate([kn_in_ref[...], kr_ref[...]], axis=-1)
        kf_ref[...] = _mla_norm_rope(k, gk_ref[...], cosk_ref[...], sink_ref[...]).astype(BF16)
        vb_ref[...] = v_ref[...].astype(BF16)

    qf = (_mla_norm_rope(q_ref[...], gq_ref[...], cosq_ref[...], sinq_ref[...]) * (MLA_QK ** -0.5)).astype(BF16)
    s = lax.dot_general(qf, kf_ref[...], _NT, preferred_element_type=F32)
    e = jnp.exp(s - jnp.max(s, axis=-1, keepdims=True))
    l = jnp.sum(e, axis=-1, keepdims=True)
    o = jnp.dot(e.astype(BF16), vb_ref[...], preferred_element_type=F32)
    o_ref[...] = (o * (1.0 / l)).astype(o_ref.dtype)


def mla_attention(q_up, kv_up, k_rope, g_qk, cos_t, sin_t, tq=256):
    b, s, _ = q_up.shape
    pad = MLA_QK_PAD - MLA_QK
    gq = jnp.pad(g_qk[0], (0, pad)).reshape(1, MLA_QK_PAD)
    gk = jnp.pad(g_qk[1], (0, pad)).reshape(1, MLA_QK_PAD)
    vec = pl.BlockSpec((1, MLA_QK_PAD), lambda bi, h, qi: (0, 0))
    rope_q = pl.BlockSpec((tq, HEAD_DIM), lambda bi, h, qi: (qi, 0))
    rope_k = pl.BlockSpec((s, HEAD_DIM), lambda bi, h, qi: (0, 0))
    return pl.pallas_call(
        _mla_attn_kernel,
        grid=(b, N_HEADS, s // tq),
        in_specs=[pl.BlockSpec((None, tq, MLA_QK_PAD), lambda bi, h, qi: (bi, qi, h)),
                  pl.BlockSpec((None, s, HEAD_DIM), lambda bi, h, qi: (bi, 0, 2 * h)),
                  pl.BlockSpec((None, s, HEAD_DIM), lambda bi, h, qi: (bi, 0, 2 * h + 1)),
                  pl.BlockSpec((None, s, HEAD_DIM), lambda bi, h, qi: (bi, 0, 0)),
                  rope_q, rope_q, rope_k, rope_k, vec, vec],
        out_specs=pl.BlockSpec((None, tq, HEAD_DIM), lambda bi, h, qi: (bi, qi, h)),
        out_shape=jax.ShapeDtypeStruct((b, s, GROUP_WIDTH), BF16),
        scratch_shapes=[pltpu.VMEM((s, MLA_QK_PAD), BF16), pltpu.VMEM((s, HEAD_DIM), BF16)],
        compiler_params=_params(("parallel", "parallel", "arbitrary")),
        name="mla_attention",
    )(q_up, kv_up, kv_up, k_rope, cos_t, sin_t, cos_t, sin_t, gq, gk)


def _top_k_rows(s, payload=None):
    n, tt = s.shape
    row = lax.broadcasted_iota(I32, (n, tt), 0)
    out_row = lax.broadcasted_iota(I32, (PEER_TOPK, tt), 0)
    vals = jnp.zeros((PEER_TOPK, tt), F32)
    idxs = jnp.zeros((PEER_TOPK, tt), I32)
    for r in range(PEER_TOPK):
        m = jnp.max(s, axis=0, keepdims=True)
        idx = jnp.min(jnp.where(s == m, row, n), axis=0, keepdims=True)
        sel = row == idx
        vals = jnp.where(out_row == r, m, vals)
        pick = idx if payload is None else jnp.max(jnp.where(sel, payload, 0), axis=0, keepdims=True)
        idxs = jnp.where(out_row == r, pick, idxs)
        s = jnp.where(sel, -jnp.inf, s)
    return vals, idxs


def _peer_route_kernel(q_ref, keys_ref, expert_ref, gate_ref):
    tops = []
    for c in range(2):
        qc = q_ref[:, c * HEAD_DIM:(c + 1) * HEAD_DIM].astype(BF16)
        sc = lax.dot_general(keys_ref[c], qc, _NT, preferred_element_type=F32)
        tops.append(_top_k_rows(sc))
    (s0, i0), (s1, i1) = tops
    cand_s = jnp.concatenate([s0[i:i + 1, :] + s1 for i in range(PEER_TOPK)], axis=0)
    cand_e = jnp.concatenate([i0[i:i + 1, :] * PEER_NKEYS + i1 for i in range(PEER_TOPK)], axis=0)
    best_s, best_e = _top_k_rows(cand_s, cand_e)
    e = jnp.exp(best_s - jnp.max(best_s, axis=0, keepdims=True))
    gate_ref[...] = e * (1.0 / jnp.sum(e, axis=0, keepdims=True))
    expert_ref[...] = best_e


def peer_route(q, sub_keys, tt=256):
    t = q.shape[0]
    n_sel = PEER_HEADS * PEER_TOPK
    return pl.pallas_call(
        _peer_route_kernel,
        grid=(t // tt, PEER_HEADS),
        in_specs=[pl.BlockSpec((tt, 2 * HEAD_DIM), lambda i, h: (i, h)),
                  pl.BlockSpec((2, PEER_NKEYS, HEAD_DIM), lambda i, h: (h, 0, 0))],
        out_specs=[pl.BlockSpec((PEER_TOPK, tt), lambda i, h: (h, i)),
                   pl.BlockSpec((PEER_TOPK, tt), lambda i, h: (h, i))],
        out_shape=[jax.ShapeDtypeStruct((n_sel, t), I32), jax.ShapeDtypeStruct((n_sel, t), F32)],
        compiler_params=_params(("parallel", "parallel")),
        name="peer_route",
    )(q, sub_keys)


def _peer_gate_matrix_kernel(expert_ref, gate_ref, o_ref, e_s, g_s):
    e_s[...] = expert_ref[...].T
    g_s[...] = gate_ref[...].T
    n_sel = e_s.shape[1]
    row = lax.broadcasted_iota(I32, (PEER_NKEYS, n_sel), 0)

    def body(t, carry):
        e_row = e_s[pl.ds(t, 1), :]
        g_row = g_s[pl.ds(t, 1), :]
        hi = jnp.where(row == (e_row >> 7), g_row, 0.0).astype(BF16)
        lo = jnp.where(row == (e_row & (PEER_NKEYS - 1)), 1.0, 0.0).astype(BF16)
        o_ref[t] = lax.dot_general(hi, lo, _NT, preferred_element_type=F32).astype(o_ref.dtype)
        return carry

    lax.fori_loop(0, e_s.shape[0], body, 0)


def peer_gate_matrix(expert_t, gate_t, tg=128):
    n_sel, t = expert_t.shape
    return pl.pallas_call(
        _peer_gate_matrix_kernel,
        grid=(t // tg,),
        in_specs=[pl.BlockSpec((n_sel, tg), lambda i: (0, i)), pl.BlockSpec((n_sel, tg), lambda i: (0, i))],
        out_specs=pl.BlockSpec((tg, PEER_NKEYS, PEER_NKEYS), lambda i: (i, 0, 0)),
        out_shape=jax.ShapeDtypeStruct((t, PEER_NKEYS, PEER_NKEYS), BF16),
        scratch_shapes=[pltpu.VMEM((tg, n_sel), I32), pltpu.VMEM((tg, n_sel), F32)],
        compiler_params=_params(("parallel",)),
        name="peer_gate_matrix",
    )(expert_t, gate_t)


def _rope_tables(s):
    inv_freq = 1.0 / (ROPE_THETA ** (np.arange(0, QK_ROPE, 2, dtype=np.float32) / QK_ROPE))
    ang = jnp.arange(s, dtype=F32)[:, None] * jnp.asarray(inv_freq, F32)[None, :]
    cos, sin = jnp.cos(ang), jnp.sin(ang)
    zeros = jnp.zeros((s, HEAD_DIM - QK_ROPE), F32)
    return (jnp.concatenate([cos, cos, zeros], axis=-1), jnp.concatenate([-sin, sin, zeros], axis=-1))


def _layer(x2, b, s, l, cos_t, sin_t, g_mix, w_in, w_out, diff_lambda, diff_g_qk, diff_g_out, dil_g_qk, na_g_qk,
           na_rpb, mla_g_cq, mla_g_ckv, mla_w_uq, mla_w_ukv, mla_g_qk, g_ffn, peer_w_q, peer_sub_keys, peer_u,
           peer_v):
    t = b * s
    lambda_init = 0.8 - 0.6 * math.exp(-0.3 * l)
    n_qkv = 3 * QKV_COLS

    h = rms_rows(x2, g_mix)
    p = matmul(h, w_in[:, :n_qkv].astype(BF16)).reshape(b, s, n_qkv)
    w_lat = jnp.pad(w_in[:, n_qkv:], ((0, 0), (0, HEAD_DIM - QK_ROPE))).astype(BF16)
    p_lat = matmul(h, w_lat, tn=MLA_LATENT_PAD)
    c_q, c_kv, k_rope = mla_latent_norm(p_lat, mla_g_cq, mla_g_ckv)

    blocks = QKV_COLS // HEAD_DIM
    o_a = diff_attention(p, 0, diff_lambda, diff_g_qk, diff_g_out, lambda_init)
    o_b = dilated_attention(p, blocks, dil_g_qk)
    o_c = neighbourhood_attention(p, 2 * blocks, na_g_qk, na_rpb)

    w_uq = jnp.pad(mla_w_uq.reshape(Q_LORA, N_HEADS, MLA_QK), ((0, 0), (0, 0), (0, MLA_QK_PAD - MLA_QK)))
    q_up = matmul(c_q, w_uq.reshape(Q_LORA, N_HEADS * MLA_QK_PAD).astype(BF16), tk=Q_LORA)
    kv_up = matmul(c_kv, mla_w_ukv.astype(BF16), tk=KV_LORA)
    o_d = mla_attention(q_up.reshape(b, s, -1), kv_up.reshape(b, s, -1), k_rope.reshape(b, s, HEAD_DIM),
                        mla_g_qk, cos_t, sin_t)

    mixed = jnp.concatenate([o_a, o_b, o_c, o_d], axis=-1).reshape(t, 4 * GROUP_WIDTH)
    x2 = matmul(mixed, w_out.astype(BF16), residual=x2)

    hf = rms_rows(x2, g_ffn)
    q = matmul(hf, peer_w_q.astype(BF16))
    keys = peer_sub_keys.reshape(2 * PEER_HEADS, PEER_NKEYS, HEAD_DIM).astype(BF16)
    expert_t, gate_t = peer_route(q, keys)
    gates = peer_gate_matrix(expert_t, gate_t).reshape(t, PEER_EXPERTS)
    act = peer_gated_act(hf, peer_u.astype(BF16), gates)
    return matmul(act, peer_v.astype(BF16), residual=x2)


def kernel(x, g_mix, w_in, w_out, diff_lambda, diff_g_qk, diff_g_out, dil_g_qk, na_g_qk, na_rpb, mla_g_cq,
           mla_g_ckv, mla_w_uq, mla_w_ukv, mla_g_qk, g_ffn, peer_w_q, peer_sub_keys, peer_u, peer_v):
    b, s, d = x.shape
    depth = g_mix.shape[0]
    cos_t, sin_t = _rope_tables(s)
    x2 = x.reshape(b * s, d)
    per_layer = (g_mix, w_in, w_out, diff_lambda, diff_g_qk, diff_g_out, dil_g_qk, na_g_qk, na_rpb, mla_g_cq,
                 mla_g_ckv, mla_w_uq, mla_w_ukv, mla_g_qk, g_ffn, peer_w_q, peer_sub_keys, peer_u, peer_v)
    for l in range(depth):
        x2 = _layer(x2, b, s, l, cos_t, sin_t, *(w[l] for w in per_layer))
    return x2.reshape(b, s, d)
```

```python
import functools
import math

import jax
import jax.numpy as jnp
import numpy as np
from jax import lax
from jax.experimental import pallas as pl
from jax.experimental.pallas import tpu as pltpu

F32 = jnp.float32
BF16 = jnp.bfloat16
I32 = jnp.int32

D_MODEL = 4096
HEAD_DIM = 128
N_HEADS = 8
GROUP_WIDTH = N_HEADS * HEAD_DIM
QKV_COLS = 3 * GROUP_WIDTH
EPS = 1e-6
NEG_INF = -1e30
DIFF_DH = HEAD_DIM // 2
DIL_HALF = 64
DIL_DILATIONS = (1, 4, 16)
GRID_W = 64
NA_ROWS = 8
NA_COLS = 16
NA_Q_ROWS = 4
NA_K_ROWS = NA_Q_ROWS + NA_ROWS
Q_LORA = 768
KV_LORA = 512
QK_NOPE = 128
QK_ROPE = 64
MLA_QK = QK_NOPE + QK_ROPE
MLA_QK_PAD = 2 * HEAD_DIM
ROPE_THETA = 10000.0
PEER_HEADS = 8
PEER_NKEYS = 128
PEER_TOPK = 16
PEER_EXPERTS = PEER_NKEYS * PEER_NKEYS
MLA_LATENT_PAD = Q_LORA + KV_LORA + HEAD_DIM

VMEM_LIMIT = 48 * 1024 * 1024

_NT = (((1,), (1,)), ((), ()))


def _params(sem):
    return pltpu.CompilerParams(dimension_semantics=sem, vmem_limit_bytes=VMEM_LIMIT)


def _rms_rows_kernel(x_ref, g_ref, o_ref):
    x = x_ref[...]
    ms = jnp.mean(x * x, axis=-1, keepdims=True)
    o_ref[...] = (x * lax.rsqrt(ms + EPS) * g_ref[...]).astype(o_ref.dtype)


def rms_rows(x, g, tm=256):
    t, c = x.shape
    return pl.pallas_call(
        _rms_rows_kernel,
        grid=(t // tm,),
        in_specs=[pl.BlockSpec((tm, c), lambda i: (i, 0)), pl.BlockSpec((1, c), lambda i: (0, 0))],
        out_specs=pl.BlockSpec((tm, c), lambda i: (i, 0)),
        out_shape=jax.ShapeDtypeStruct((t, c), BF16),
        compiler_params=_params(("parallel",)),
        name="rms_rows",
    )(x, g.reshape(1, c))


def _mla_latent_kernel(p_ref, gq_ref, gkv_ref, cq_ref, ckv_ref, kr_ref):
    cq = p_ref[:, :Q_LORA]
    ckv = p_ref[:, Q_LORA:Q_LORA + KV_LORA]
    cq_ref[...] = (cq * lax.rsqrt(jnp.mean(cq * cq, axis=-1, keepdims=True) + EPS) * gq_ref[...]).astype(BF16)
    ckv_ref[...] = (ckv * lax.rsqrt(jnp.mean(ckv * ckv, axis=-1, keepdims=True) + EPS) * gkv_ref[...]).astype(BF16)
    kr_ref[...] = p_ref[:, Q_LORA + KV_LORA:]


def mla_latent_norm(p_lat, g_cq, g_ckv, tm=512):
    t = p_lat.shape[0]
    return pl.pallas_call(
        _mla_latent_kernel,
        grid=(t // tm,),
        in_specs=[pl.BlockSpec((tm, MLA_LATENT_PAD), lambda i: (i, 0)),
                  pl.BlockSpec((1, Q_LORA), lambda i: (0, 0)),
                  pl.BlockSpec((1, KV_LORA), lambda i: (0, 0))],
        out_specs=[pl.BlockSpec((tm, Q_LORA), lambda i: (i, 0)),
                   pl.BlockSpec((tm, KV_LORA), lambda i: (i, 0)),
                   pl.BlockSpec((tm, HEAD_DIM), lambda i: (i, 0))],
        out_shape=[jax.ShapeDtypeStruct((t, Q_LORA), BF16),
                   jax.ShapeDtypeStruct((t, KV_LORA), BF16),
                   jax.ShapeDtypeStruct((t, HEAD_DIM), F32)],
        compiler_params=_params(("parallel",)),
        name="mla_latent_norm",
    )(p_lat, g_cq.reshape(1, Q_LORA), g_ckv.reshape(1, KV_LORA))


def _mm_kernel(a_ref, b_ref, o_ref):
    o_ref[...] = jnp.dot(a_ref[...], b_ref[...], preferred_element_type=F32).astype(o_ref.dtype)


def _mm_res_kernel(a_ref, b_ref, r_ref, o_ref):
    o_ref[...] = r_ref[...] + jnp.dot(a_ref[...], b_ref[...], preferred_element_type=F32)


def _mm_res_acc_kernel(a_ref, b_ref, r_ref, o_ref):
    @pl.when(pl.program_id(2) == 0)
    def _():
        o_ref[...] = r_ref[...]

    o_ref[...] += jnp.dot(a_ref[...], b_ref[...], preferred_element_type=F32)


def matmul(a, b, *, out_dtype=F32, residual=None, tm=512, tn=1024):
    m, kd = a.shape
    n = b.shape[1]
    tm, tn = min(tm, m), min(tn, n)
    in_specs = [pl.BlockSpec((tm, kd), lambda j, i: (i, 0)),
                pl.BlockSpec((kd, tn), lambda j, i: (0, j))]
    args = [a, b]
    body = _mm_kernel
    if residual is not None:
        body = _mm_res_kernel
        in_specs.append(pl.BlockSpec((tm, tn), lambda j, i: (i, j)))
        args.append(residual)
    return pl.pallas_call(
        body,
        grid=(n // tn, m // tm),
        in_specs=in_specs,
        out_specs=pl.BlockSpec((tm, tn), lambda j, i: (i, j)),
        out_shape=jax.ShapeDtypeStruct((m, n), out_dtype),
        compiler_params=_params(("parallel", "parallel")),
        name="matmul_res" if residual is not None else "matmul",
    )(*args)


def matmul_res_ktiled(a, b, residual, tm=1024, tn=1024, tk=2048):
    m, kd = a.shape
    n = b.shape[1]
    return pl.pallas_call(
        _mm_res_acc_kernel,
        grid=(m // tm, n // tn, kd // tk),
        in_specs=[pl.BlockSpec((tm, tk), lambda i, j, k: (i, k)),
                  pl.BlockSpec((tk, tn), lambda i, j, k: (k, j)),
                  pl.BlockSpec((tm, tn), lambda i, j, k: (i, j))],
        out_specs=pl.BlockSpec((tm, tn), lambda i, j, k: (i, j)),
        out_shape=jax.ShapeDtypeStruct((m, n), F32),
        compiler_params=_params(("parallel", "parallel", "arbitrary")),
        name="matmul_res_ktiled",
    )(a, b, residual)


def _gelu_exact(x):
    return 0.5 * x * (1.0 + lax.erf(x * (1.0 / math.sqrt(2.0))))


def _peer_act_kernel(h_ref, u_ref, g_ref, o_ref):
    act = _gelu_exact(lax.dot_general(h_ref[...], u_ref[...], _NT, preferred_element_type=F32))
    for a in range(g_ref.shape[1]):
        cols = slice(a * PEER_NKEYS, (a + 1) * PEER_NKEYS)
        o_ref[:, cols] = (act[:, cols] * g_ref[:, a, :]).astype(o_ref.dtype)


def peer_gated_act(hn, u, gates, tm=512, tn=1024):
    t, d = hn.shape
    e = u.shape[0]
    tm = min(tm, t)
    return pl.pallas_call(
        _peer_act_kernel,
        grid=(e // tn, t // tm),
        in_specs=[pl.BlockSpec((tm, d), lambda j, i: (i, 0)),
                  pl.BlockSpec((tn, d), lambda j, i: (j, 0)),
                  pl.BlockSpec((tm, tn // PEER_NKEYS, PEER_NKEYS), lambda j, i: (i, j, 0))],
        out_specs=pl.BlockSpec((tm, tn), lambda j, i: (i, j)),
        out_shape=jax.ShapeDtypeStruct((t, e), BF16),
        compiler_params=_params(("parallel", "parallel")),
        name="peer_gated_act",
    )(hn, u, gates)


def _rms_lanes(x, g):
    ms = jnp.mean(x * x, axis=-1, keepdims=True)
    return x * lax.rsqrt(ms + EPS) * g


def _rms_halves(x, g):
    lane = lax.broadcasted_iota(I32, x.shape, 1)
    lo = lane < DIFF_DH
    xx = x * x
    ms_lo = jnp.sum(jnp.where(lo, xx, 0.0), axis=-1, keepdims=True) * (1.0 / DIFF_DH)
    ms_hi = jnp.sum(jnp.where(lo, 0.0, xx), axis=-1, keepdims=True) * (1.0 / DIFF_DH)
    inv = jnp.where(lo, lax.rsqrt(ms_lo + EPS), lax.rsqrt(ms_hi + EPS))
    return x * inv * g


def _rel_pos(tq, s, q0):
    return lax.broadcasted_iota(I32, (tq, s), 1) - lax.broadcasted_iota(I32, (tq, s), 0) - q0


def _diff_attn_kernel(slope_ref, q_ref, k_ref, v_ref, gq_ref, gk_ref, go_ref, lam_ref, o_ref, kn_ref, vb_ref,
                      *, tq, lambda_init):
    h = pl.program_id(1)
    qi = pl.program_id(2)
    s_len = k_ref.shape[0]

    @pl.when(qi == 0)
    def _():
        kn_ref[...] = _rms_halves(k_ref[...], gk_ref[...]).astype(BF16)
        vb_ref[...] = v_ref[...].astype(BF16)

    lv = lam_ref[...]
    lam = (jnp.exp(jnp.sum(lv[0:1] * lv[1:2], axis=-1, keepdims=True))
           - jnp.exp(jnp.sum(lv[2:3] * lv[3:4], axis=-1, keepdims=True)) + lambda_init)

    qn = _rms_halves(q_ref[...], gq_ref[...]) * (DIFF_DH ** -0.5)
    lane = lax.broadcasted_iota(I32, qn.shape, 1)
    q0 = jnp.where(lane < DIFF_DH, qn, 0.0).astype(BF16)
    q1 = jnp.where(lane < DIFF_DH, 0.0, qn).astype(BF16)
    kn = kn_ref[...]
    bias = slope_ref[h] * jnp.abs(_rel_pos(tq, s_len, qi * tq)).astype(F32)

    def softmax_parts(qc):
        s = lax.dot_general(qc, kn, _NT, preferred_element_type=F32) - bias
        e = jnp.exp(s - jnp.max(s, axis=-1, keepdims=True))
        return e, jnp.sum(e, axis=-1, keepdims=True)

    e0, l0 = softmax_parts(q0)
    e1, l1 = softmax_parts(q1)
    w = e0 * (1.0 / l0) - e1 * (lam / l1)
    o = jnp.dot(w.astype(BF16), vb_ref[...], preferred_element_type=F32)
    o_ref[...] = (_rms_lanes(o, go_ref[...]) * (1.0 - lambda_init)).astype(o_ref.dtype)


def diff_attention(p, col0, lam_vecs, g_qk, g_out, lambda_init, tq=256):
    b, s, _ = p.shape
    slopes = jnp.asarray(2.0 ** (-8.0 * np.arange(1, N_HEADS + 1) / N_HEADS), F32)
    lam_pad = jnp.pad(lam_vecs, ((0, 0), (0, HEAD_DIM - DIFF_DH)))
    gq = jnp.tile(g_qk[0], 2).reshape(1, HEAD_DIM)
    gk = jnp.tile(g_qk[1], 2).reshape(1, HEAD_DIM)
    vec = pl.BlockSpec((1, HEAD_DIM), lambda bi, h, qi: (0, 0))
    return pl.pallas_call(
        functools.partial(_diff_attn_kernel, tq=tq, lambda_init=lambda_init),
        grid=(b, N_HEADS, s // tq),
        in_specs=[pl.BlockSpec(memory_space=pltpu.SMEM),
                  pl.BlockSpec((None, tq, HEAD_DIM), lambda bi, h, qi: (bi, qi, col0 + h)),
                  pl.BlockSpec((None, s, HEAD_DIM), lambda bi, h, qi: (bi, 0, col0 + N_HEADS + h)),
                  pl.BlockSpec((None, s, HEAD_DIM), lambda bi, h, qi: (bi, 0, col0 + 2 * N_HEADS + h)),
                  vec, vec, vec,
                  pl.BlockSpec((4, HEAD_DIM), lambda bi, h, qi: (0, 0))],
        out_specs=pl.BlockSpec((None, tq, HEAD_DIM), lambda bi, h, qi: (bi, qi, h)),
        out_shape=jax.ShapeDtypeStruct((b, s, GROUP_WIDTH), BF16),
        scratch_shapes=[pltpu.VMEM((s, HEAD_DIM), BF16), pltpu.VMEM((s, HEAD_DIM), BF16)],
        compiler_params=_params(("parallel", "parallel", "arbitrary")),
        name="diff_attention",
    )(slopes, p, p, p, gq, gk, g_out.reshape(1, HEAD_DIM), lam_pad)


def _dilated_attn_kernel(slope_ref, q_ref, k_ref, v_ref, gq_ref, gk_ref, o_ref, kn_ref, vb_ref, *, tq):
    h = pl.program_id(1)
    qi = pl.program_id(2)
    s_len = k_ref.shape[0]

    @pl.when(qi == 0)
    def _():
        kn_ref[...] = _rms_lanes(k_ref[...], gk_ref[...]).astype(BF16)
        vb_ref[...] = v_ref[...].astype(BF16)

    qn = (_rms_lanes(q_ref[...], gq_ref[...]) * (HEAD_DIM ** -0.5)).astype(BF16)
    d = _rel_pos(tq, s_len, qi * tq)
    ad = jnp.abs(d)
    mult = jnp.zeros((tq, s_len), F32)
    for dil in DIL_DILATIONS:
        mult = mult + jnp.where(((d & (dil - 1)) == 0) & (ad <= DIL_HALF * dil), 1.0, 0.0)
    s = lax.dot_general(qn, kn_ref[...], _NT, preferred_element_type=F32) - slope_ref[h] * ad.astype(F32)
    s = jnp.where(mult > 0.0, s, NEG_INF)
    e = jnp.exp(s - jnp.max(s, axis=-1, keepdims=True)) * mult
    l = jnp.sum(e, axis=-1, keepdims=True)
    o = jnp.dot(e.astype(BF16), vb_ref[...], preferred_element_type=F32)
    o_ref[...] = (o * (1.0 / l)).astype(o_ref.dtype)


def dilated_attention(p, col0, g_qk, tq=256):
    b, s, _ = p.shape
    slopes = jnp.asarray(2.0 ** (-8.0 * np.arange(1, N_HEADS + 1) / N_HEADS), F32)
    vec = pl.BlockSpec((1, HEAD_DIM), lambda bi, h, qi: (0, 0))
    return pl.pallas_call(
        functools.partial(_dilated_attn_kernel, tq=tq),
        grid=(b, N_HEADS, s // tq),
        in_specs=[pl.BlockSpec(memory_space=pltpu.SMEM),
                  pl.BlockSpec((None, tq, HEAD_DIM), lambda bi, h, qi: (bi, qi, col0 + h)),
                  pl.BlockSpec((None, s, HEAD_DIM), lambda bi, h, qi: (bi, 0, col0 + N_HEADS + h)),
                  pl.BlockSpec((None, s, HEAD_DIM), lambda bi, h, qi: (bi, 0, col0 + 2 * N_HEADS + h)),
                  vec, vec],
        out_specs=pl.BlockSpec((None, tq, HEAD_DIM), lambda bi, h, qi: (bi, qi, h)),
        out_shape=jax.ShapeDtypeStruct((b, s, GROUP_WIDTH), BF16),
        scratch_shapes=[pltpu.VMEM((s, HEAD_DIM), BF16), pltpu.VMEM((s, HEAD_DIM), BF16)],
        compiler_params=_params(("parallel", "parallel", "arbitrary")),
        name="dilated_attention",
    )(slopes, p, p, p, g_qk[0].reshape(1, HEAD_DIM), g_qk[1].reshape(1, HEAD_DIM))


def _na_window_start(q_row0, n_rows):
    return jnp.clip(q_row0 - NA_ROWS // 2, 0, n_rows - NA_K_ROWS)


def _na_attn_kernel(q_ref, k_ref, v_ref, gq_ref, gk_ref, bias_ref, o_ref, kn_ref, vb_ref):
    qi = pl.program_id(2)
    n_rows = k_ref.shape[0] // GRID_W

    @pl.when(qi == 0)
    def _():
        kn_ref[...] = _rms_lanes(k_ref[...], gk_ref[...]).astype(BF16)
        vb_ref[...] = v_ref[...].astype(BF16)

    k0 = pl.multiple_of(_na_window_start(qi * NA_Q_ROWS, n_rows) * GRID_W, GRID_W)
    kw = kn_ref[pl.ds(k0, NA_K_ROWS * GRID_W), :]
    vw = vb_ref[pl.ds(k0, NA_K_ROWS * GRID_W), :]
    qn = (_rms_lanes(q_ref[...], gq_ref[...]) * (HEAD_DIM ** -0.5)).astype(BF16)
    s = lax.dot_general(qn, kw, _NT, preferred_element_type=F32) + bias_ref[...]
    e = jnp.exp(s - jnp.max(s, axis=-1, keepdims=True))
    l = jnp.sum(e, axis=-1, keepdims=True)
    o = jnp.dot(e.astype(BF16), vw, preferred_element_type=F32)
    o_ref[...] = (o * (1.0 / l)).astype(o_ref.dtype)


def _na_bias_kernel(t_ref, o_ref, *, n_rows):
    q_row0 = pl.program_id(1) * NA_Q_ROWS
    k_row0 = _na_window_start(q_row0, n_rows)
    lane = lax.broadcasted_iota(I32, (GRID_W, 2 * GRID_W), 1)
    for a in range(NA_Q_ROWS):
        rq = q_row0 + a
        rs = jnp.clip(rq - NA_ROWS // 2, 0, n_rows - NA_ROWS)
        for wp in range(NA_K_ROWS // 2):
            rk = k_row0 + 2 * wp
            ok0 = ((rk >= rs) & (rk < rs + NA_ROWS)).astype(I32)
            ok1 = ((rk + 1 >= rs) & (rk + 1 < rs + NA_ROWS)).astype(I32)
            m = jnp.clip(rk - rq + NA_ROWS, 0, 2 * NA_ROWS - 1)
            ok = jnp.where(lane < GRID_W, ok0, ok1) > 0
            o_ref[a * GRID_W:(a + 1) * GRID_W, wp * 2 * GRID_W:(wp + 1) * 2 * GRID_W] = jnp.where(
                ok, t_ref[m], NEG_INF)


def _na_bias(rpb, n_rows):
    n_rel_r, n_rel_c = 2 * NA_ROWS - 1, 2 * NA_COLS - 1
    pad = GRID_W - NA_COLS
    rp = jnp.pad(rpb, ((0, 0), (0, 0), (pad, pad)))
    toe = jnp.stack([rp[:, :, NA_COLS - 1 + pad - cq:NA_COLS - 1 + pad - cq + GRID_W] for cq in range(GRID_W)],
                    axis=2)
    cq = np.arange(GRID_W)
    cs = np.clip(cq - NA_COLS // 2, 0, GRID_W - NA_COLS)
    col_ok = (cq[None, :] >= cs[:, None]) & (cq[None, :] < cs[:, None] + NA_COLS)
    toe = jnp.where(col_ok[None, None], toe, NEG_INF)
    ext = jnp.pad(toe, ((0, 0), (1, 1), (0, 0), (0, 0)), constant_values=NEG_INF)
    pairs = jnp.concatenate([ext[:, :n_rel_r + 1], ext[:, 1:]], axis=-1)
    tq, tk = NA_Q_ROWS * GRID_W, NA_K_ROWS * GRID_W
    nqb = n_rows // NA_Q_ROWS
    return pl.pallas_call(
        functools.partial(_na_bias_kernel, n_rows=n_rows),
        grid=(N_HEADS, nqb),
        in_specs=[pl.BlockSpec((None, n_rel_r + 1, GRID_W, 2 * GRID_W), lambda h, qi: (h, 0, 0, 0))],
        out_specs=pl.BlockSpec((None, None, tq, tk), lambda h, qi: (h, qi, 0, 0)),
        out_shape=jax.ShapeDtypeStruct((N_HEADS, nqb, tq, tk), F32),
        compiler_params=_params(("parallel", "parallel")),
        name="na_bias",
    )(pairs)


def neighbourhood_attention(p, col0, g_qk, rpb):
    b, s, _ = p.shape
    n_rows = s // GRID_W
    tq = NA_Q_ROWS * GRID_W
    tk = NA_K_ROWS * GRID_W
    bias = _na_bias(rpb, n_rows)
    vec = pl.BlockSpec((1, HEAD_DIM), lambda bi, h, qi: (0, 0))
    return pl.pallas_call(
        _na_attn_kernel,
        grid=(b, N_HEADS, s // tq),
        in_specs=[pl.BlockSpec((None, tq, HEAD_DIM), lambda bi, h, qi: (bi, qi, col0 + h)),
                  pl.BlockSpec((None, s, HEAD_DIM), lambda bi, h, qi: (bi, 0, col0 + N_HEADS + h)),
                  pl.BlockSpec((None, s, HEAD_DIM), lambda bi, h, qi: (bi, 0, col0 + 2 * N_HEADS + h)),
                  vec, vec,
                  pl.BlockSpec((None, None, tq, tk), lambda bi, h, qi: (h, qi, 0, 0))],
        out_specs=pl.BlockSpec((None, tq, HEAD_DIM), lambda bi, h, qi: (bi, qi, h)),
        out_shape=jax.ShapeDtypeStruct((b, s, GROUP_WIDTH), BF16),
        scratch_shapes=[pltpu.VMEM((s, HEAD_DIM), BF16), pltpu.VMEM((s, HEAD_DIM), BF16)],
        compiler_params=_params(("parallel", "parallel", "arbitrary")),
        name="neighbourhood_attention",
    )(p, p, p, g_qk[0].reshape(1, HEAD_DIM), g_qk[1].reshape(1, HEAD_DIM), bias)


def _mla_norm_rope(x, g, cos, sin):
    ms = jnp.sum(x * x, axis=-1, keepdims=True) * (1.0 / MLA_QK)
    xn = x * lax.rsqrt(ms + EPS) * g
    xr = xn[:, HEAD_DIM:]
    partner = pltpu.roll(xr, QK_ROPE // 2, 1) + pltpu.roll(xr, HEAD_DIM - QK_ROPE // 2, 1)
    return jnp.concatenate([xn[:, :HEAD_DIM], xr * cos + partner * sin], axis=-1)


def _mla_attn_kernel(q_ref, kn_in_ref, v_ref, kr_ref, cosq_ref, sinq_ref, cosk_ref, sink_ref, gq_ref, gk_ref,
                     o_ref, kf_ref, vb_ref):
    qi = pl.program_id(2)

    @pl.when(qi == 0)
    def _():
        k = jnp.concatenate([kn_in_ref[...], kr_ref[...]], axis=-1)
        kf_ref[...] = _mla_norm_rope(k, gk_ref[...], cosk_ref[...], sink_ref[...]).astype(BF16)
        vb_ref[...] = v_ref[...].astype(BF16)

    qf = (_mla_norm_rope(q_ref[...], gq_ref[...], cosq_ref[...], sinq_ref[...]) * (MLA_QK ** -0.5)).astype(BF16)
    s = lax.dot_general(qf, kf_ref[...], _NT, preferred_element_type=F32)
    e = jnp.exp(s - jnp.max(s, axis=-1, keepdims=True))
    l = jnp.sum(e, axis=-1, keepdims=True)
    o = jnp.dot(e.astype(BF16), vb_ref[...], preferred_element_type=F32)
    o_ref[...] = (o * (1.0 / l)).astype(o_ref.dtype)


def mla_attention(q_up, kv_up, k_rope, g_qk, cos_t, sin_t, tq=256):
    b, s, _ = q_up.shape
    pad = MLA_QK_PAD - MLA_QK
    gq = jnp.pad(g_qk[0], (0, pad)).reshape(1, MLA_QK_PAD)
    gk = jnp.pad(g_qk[1], (0, pad)).reshape(1, MLA_QK_PAD)
    vec = pl.BlockSpec((1, MLA_QK_PAD), lambda bi, h, qi: (0, 0))
    rope_q = pl.BlockSpec((tq, HEAD_DIM), lambda bi, h, qi: (qi, 0))
    rope_k = pl.BlockSpec((s, HEAD_DIM), lambda bi, h, qi: (0, 0))
    return pl.pallas_call(
        _mla_attn_kernel,
        grid=(b, N_HEADS, s // tq),
        in_specs=[pl.BlockSpec((None, tq, MLA_QK_PAD), lambda bi, h, qi: (bi, qi, h)),
                  pl.BlockSpec((None, s, HEAD_DIM), lambda bi, h, qi: (bi, 0, 2 * h)),
                  pl.BlockSpec((None, s, HEAD_DIM), lambda bi, h, qi: (bi, 0, 2 * h + 1)),
                  pl.BlockSpec((None, s, HEAD_DIM), lambda bi, h, qi: (bi, 0, 0)),
                  rope_q, rope_q, rope_k, rope_k, vec, vec],
        out_specs=pl.BlockSpec((None, tq, HEAD_DIM), lambda bi, h, qi: (bi, qi, h)),
        out_shape=jax.ShapeDtypeStruct((b, s, GROUP_WIDTH), BF16),
        scratch_shapes=[pltpu.VMEM((s, MLA_QK_PAD), BF16), pltpu.VMEM((s, HEAD_DIM), BF16)],
        compiler_params=_params(("parallel", "parallel", "arbitrary")),
        name="mla_attention",
    )(q_up, kv_up, kv_up, k_rope, cos_t, sin_t, cos_t, sin_t, gq, gk)


def _top_k_rows(s, order, payload=None):
    n, tt = s.shape
    out_row = lax.broadcasted_iota(I32, (PEER_TOPK, tt), 0)
    vals = jnp.zeros((PEER_TOPK, tt), F32)
    picks = jnp.zeros((PEER_TOPK, tt), I32)
    for r in range(PEER_TOPK):
        m = jnp.max(s, axis=0, keepdims=True)
        first = jnp.min(jnp.where(s == m, order, jnp.iinfo(jnp.int32).max), axis=0, keepdims=True)
        sel = order == first
        vals = jnp.where(out_row == r, m, vals)
        pick = first if payload is None else jnp.max(jnp.where(sel, payload, 0), axis=0, keepdims=True)
        picks = jnp.where(out_row == r, pick, picks)
        s = jnp.where(sel, -jnp.inf, s)
    return vals, picks


_PAIR_BLOCKS = (("i", 16, 0), ("i", 8, 1), ("i", 8, 2), ("i", 8, 3), ("j", 16, 0), ("j", 8, 1), ("j", 8, 2))
_PAIR_SPLIT = 4


def _pair_candidates(s0, i0, s1, i1):
    tt = s0.shape[1]
    scores, experts, flats = [], [], []
    for vary, n, fixed in _PAIR_BLOCKS:
        r = lax.broadcasted_iota(I32, (n, tt), 0)
        limit = PEER_TOPK // (fixed + 1)
        if vary == "i":
            sc = s0[:n] + s1[fixed:fixed + 1]
            ex = i0[:n] * PEER_NKEYS + i1[fixed:fixed + 1]
            flat = r * PEER_TOPK + fixed
            valid = r < limit
        else:
            sc = s0[fixed:fixed + 1] + s1[:n]
            ex = i0[fixed:fixed + 1] * PEER_NKEYS + i1[:n]
            flat = fixed * PEER_TOPK + r
            valid = (r < limit) & (r >= _PAIR_SPLIT)
        scores.append(jnp.where(valid, sc, -jnp.inf))
        experts.append(ex)
        flats.append(flat)
    return jnp.concatenate(scores, 0), jnp.concatenate(experts, 0), jnp.concatenate(flats, 0)


def _peer_route_kernel(q_ref, keys_ref, expert_ref, gate_ref):
    tt = q_ref.shape[0]
    lanes = HEAD_DIM
    key_row = lax.broadcasted_iota(I32, (PEER_NKEYS, lanes), 0)
    for c0 in range(0, tt, lanes):
        tops = []
        for c in range(2):
            qc = q_ref[c0:c0 + lanes, c * HEAD_DIM:(c + 1) * HEAD_DIM].astype(BF16)
            sc = lax.dot_general(keys_ref[c], qc, _NT, preferred_element_type=F32)
            tops.append(_top_k_rows(sc, key_row))
        (s0, i0), (s1, i1) = tops
        cand_s, cand_e, cand_flat = _pair_candidates(s0, i0, s1, i1)
        best_s, best_e = _top_k_rows(cand_s, cand_flat, cand_e)
        e = jnp.exp(best_s - jnp.max(best_s, axis=0, keepdims=True))
        gate_ref[:, c0:c0 + lanes] = e * (1.0 / jnp.sum(e, axis=0, keepdims=True))
        expert_ref[:, c0:c0 + lanes] = best_e


def peer_route(q, sub_keys, tt=256):
    t = q.shape[0]
    n_sel = PEER_HEADS * PEER_TOPK
    return pl.pallas_call(
        _peer_route_kernel,
        grid=(t // tt, PEER_HEADS),
        in_specs=[pl.BlockSpec((tt, 2 * HEAD_DIM), lambda i, h: (i, h)),
                  pl.BlockSpec((2, PEER_NKEYS, HEAD_DIM), lambda i, h: (h, 0, 0))],
        out_specs=[pl.BlockSpec((PEER_TOPK, tt), lambda i, h: (h, i)),
                   pl.BlockSpec((PEER_TOPK, tt), lambda i, h: (h, i))],
        out_shape=[jax.ShapeDtypeStruct((n_sel, t), I32), jax.ShapeDtypeStruct((n_sel, t), F32)],
        compiler_params=_params(("parallel", "parallel")),
        name="peer_route",
    )(q, sub_keys)


def _peer_gate_matrix_kernel(expert_ref, gate_ref, o_ref, e_s, g_s):
    e_s[...] = expert_ref[...].T
    g_s[...] = gate_ref[...].T
    n_sel = e_s.shape[1]
    row = lax.broadcasted_iota(I32, (PEER_NKEYS, n_sel), 0)

    def body(t, carry):
        e_row = e_s[pl.ds(t, 1), :]
        g_row = g_s[pl.ds(t, 1), :]
        hi = jnp.where(row == (e_row >> 7), g_row, 0.0).astype(BF16)
        lo = jnp.where(row == (e_row & (PEER_NKEYS - 1)), 1.0, 0.0).astype(BF16)
        o_ref[t] = lax.dot_general(hi, lo, _NT, preferred_element_type=F32)
        return carry

    lax.fori_loop(0, e_s.shape[0], body, 0, unroll=8)


def peer_gate_matrix(expert_t, gate_t, tg=128):
    n_sel, t = expert_t.shape
    return pl.pallas_call(
        _peer_gate_matrix_kernel,
        grid=(t // tg,),
        in_specs=[pl.BlockSpec((n_sel, tg), lambda i: (0, i)), pl.BlockSpec((n_sel, tg), lambda i: (0, i))],
        out_specs=pl.BlockSpec((tg, PEER_NKEYS, PEER_NKEYS), lambda i: (i, 0, 0)),
        out_shape=jax.ShapeDtypeStruct((t, PEER_NKEYS, PEER_NKEYS), F32),
        scratch_shapes=[pltpu.VMEM((tg, n_sel), I32), pltpu.VMEM((tg, n_sel), F32)],
        compiler_params=_params(("parallel",)),
        name="peer_gate_matrix",
    )(expert_t, gate_t)


def _rope_tables(s):
    inv_freq = 1.0 / (ROPE_THETA ** (np.arange(0, QK_ROPE, 2, dtype=np.float32) / QK_ROPE))
    ang = jnp.arange(s, dtype=F32)[:, None] * jnp.asarray(inv_freq, F32)[None, :]
    cos, sin = jnp.cos(ang), jnp.sin(ang)
    zeros = jnp.zeros((s, HEAD_DIM - QK_ROPE), F32)
    return (jnp.concatenate([cos, cos, zeros], axis=-1), jnp.concatenate([-sin, sin, zeros], axis=-1))


def _layer(x2, b, s, l, cos_t, sin_t, g_mix, w_in, w_out, diff_lambda, diff_g_qk, diff_g_out, dil_g_qk, na_g_qk,
           na_rpb, mla_g_cq, mla_g_ckv, mla_w_uq, mla_w_ukv, mla_g_qk, g_ffn, peer_w_q, peer_sub_keys, peer_u,
           peer_v):
    t = b * s
    lambda_init = 0.8 - 0.6 * math.exp(-0.3 * l)
    n_qkv = 3 * QKV_COLS

    h = rms_rows(x2, g_mix)
    p = matmul(h, w_in[:, :n_qkv].astype(BF16)).reshape(b, s, n_qkv)
    w_lat = jnp.pad(w_in[:, n_qkv:], ((0, 0), (0, HEAD_DIM - QK_ROPE))).astype(BF16)
    p_lat = matmul(h, w_lat, tn=MLA_LATENT_PAD)
    c_q, c_kv, k_rope = mla_latent_norm(p_lat, mla_g_cq, mla_g_ckv)

    blocks = QKV_COLS // HEAD_DIM
    o_a = diff_attention(p, 0, diff_lambda, diff_g_qk, diff_g_out, lambda_init)
    o_b = dilated_attention(p, blocks, dil_g_qk)
    o_c = neighbourhood_attention(p, 2 * blocks, na_g_qk, na_rpb)

    w_uq = jnp.pad(mla_w_uq.reshape(Q_LORA, N_HEADS, MLA_QK), ((0, 0), (0, 0), (0, MLA_QK_PAD - MLA_QK)))
    q_up = matmul(c_q, w_uq.reshape(Q_LORA, N_HEADS * MLA_QK_PAD).astype(BF16), tm=1024)
    kv_up = matmul(c_kv, mla_w_ukv.astype(BF16), tm=1024)
    o_d = mla_attention(q_up.reshape(b, s, -1), kv_up.reshape(b, s, -1), k_rope.reshape(b, s, HEAD_DIM),
                        mla_g_qk, cos_t, sin_t)

    mixed = jnp.concatenate([o_a, o_b, o_c, o_d], axis=-1).reshape(t, 4 * GROUP_WIDTH)
    x2 = matmul(mixed, w_out.astype(BF16), residual=x2)

    hf = rms_rows(x2, g_ffn)
    q = matmul(hf, peer_w_q.astype(BF16))
    keys = peer_sub_keys.reshape(2 * PEER_HEADS, PEER_NKEYS, HEAD_DIM).astype(BF16)
    expert_t, gate_t = peer_route(q, keys)
    gates = peer_gate_matrix(expert_t, gate_t)
    act = peer_gated_act(hf, peer_u.astype(BF16), gates)
    return matmul_res_ktiled(act, peer_v.astype(BF16), x2)


def kernel(x, g_mix, w_in, w_out, diff_lambda, diff_g_qk, diff_g_out, dil_g_qk, na_g_qk, na_rpb, mla_g_cq,
           mla_g_ckv, mla_w_uq, mla_w_ukv, mla_g_qk, g_ffn, peer_w_q, peer_sub_keys, peer_u, peer_v):
    b, s, d = x.shape
    depth = g_mix.shape[0]
    cos_t, sin_t = _rope_tables(s)
    x2 = x.reshape(b * s, d)
    per_layer = (g_mix, w_in, w_out, diff_lambda, diff_g_qk, diff_g_out, dil_g_qk, na_g_qk, na_rpb, mla_g_cq,
                 mla_g_ckv, mla_w_uq, mla_w_ukv, mla_g_qk, g_ffn, peer_w_q, peer_sub_keys, peer_u, peer_v)
    for l in range(depth):
        x2 = _layer(x2, b, s, l, cos_t, sin_t, *(w[l] for w in per_layer))
    return x2.reshape(b, s, d)
```

```python
import functools
import math

import jax
import jax.numpy as jnp
import numpy as np
from jax import lax
from jax.experimental import pallas as pl
from jax.experimental.pallas import tpu as pltpu

F32 = jnp.float32
BF16 = jnp.bfloat16
I32 = jnp.int32

D_MODEL = 4096
HEAD_DIM = 128
N_HEADS = 8
GROUP_WIDTH = N_HEADS * HEAD_DIM
QKV_COLS = 3 * GROUP_WIDTH
EPS = 1e-6
NEG_INF = -1e30
LOG2E = math.log2(math.e)
DIFF_DH = HEAD_DIM // 2
DIL_HALF = 64
DIL_DILATIONS = (1, 4, 16)
DIL_REACH = DIL_HALF * max(DIL_DILATIONS)
GRID_W = 64
NA_ROWS = 8
NA_COLS = 16
NA_Q_ROWS = 4
NA_K_ROWS = NA_Q_ROWS + NA_ROWS
Q_LORA = 768
KV_LORA = 512
QK_NOPE = 128
QK_ROPE = 64
MLA_QK = QK_NOPE + QK_ROPE
MLA_QK_PAD = 2 * HEAD_DIM
ROPE_THETA = 10000.0
PEER_HEADS = 8
PEER_NKEYS = 128
PEER_TOPK = 16
PEER_EXPERTS = PEER_NKEYS * PEER_NKEYS
MLA_LATENT_PAD = Q_LORA + KV_LORA + HEAD_DIM
ATT_ROWS = 256
TAB_ROWS = 512

VMEM_LIMIT = 48 * 1024 * 1024

_NT = (((1,), (1,)), ((), ()))


def _params(grid_rank):
    return pltpu.CompilerParams(dimension_semantics=("arbitrary",) * grid_rank, vmem_limit_bytes=VMEM_LIMIT)


def _rms_rows_kernel(x_ref, g_ref, o_ref):
    x = x_ref[...]
    ms = jnp.mean(x * x, axis=-1, keepdims=True)
    o_ref[...] = (x * lax.rsqrt(ms + EPS) * g_ref[...]).astype(o_ref.dtype)


def rms_rows(x, g, tm=256):
    t, c = x.shape
    return pl.pallas_call(
        _rms_rows_kernel,
        grid=(t // tm,),
        in_specs=[pl.BlockSpec((tm, c), lambda i: (i, 0)), pl.BlockSpec((1, c), lambda i: (0, 0))],
        out_specs=pl.BlockSpec((tm, c), lambda i: (i, 0)),
        out_shape=jax.ShapeDtypeStruct((t, c), BF16),
        compiler_params=_params(1),
        name="rms_rows",
    )(x, g.reshape(1, c))


def _mla_latent_kernel(p_ref, gq_ref, gkv_ref, cq_ref, ckv_ref, kr_ref):
    cq = p_ref[:, :Q_LORA]
    ckv = p_ref[:, Q_LORA:Q_LORA + KV_LORA]
    cq_ref[...] = (cq * lax.rsqrt(jnp.mean(cq * cq, axis=-1, keepdims=True) + EPS) * gq_ref[...]).astype(BF16)
    ckv_ref[...] = (ckv * lax.rsqrt(jnp.mean(ckv * ckv, axis=-1, keepdims=True) + EPS) * gkv_ref[...]).astype(BF16)
    kr_ref[...] = p_ref[:, Q_LORA + KV_LORA:]


def mla_latent_norm(p_lat, g_cq, g_ckv, tm=512):
    t = p_lat.shape[0]
    return pl.pallas_call(
        _mla_latent_kernel,
        grid=(t // tm,),
        in_specs=[pl.BlockSpec((tm, MLA_LATENT_PAD), lambda i: (i, 0)),
                  pl.BlockSpec((1, Q_LORA), lambda i: (0, 0)),
                  pl.BlockSpec((1, KV_LORA), lambda i: (0, 0))],
        out_specs=[pl.BlockSpec((tm, Q_LORA), lambda i: (i, 0)),
                   pl.BlockSpec((tm, KV_LORA), lambda i: (i, 0)),
                   pl.BlockSpec((tm, HEAD_DIM), lambda i: (i, 0))],
        out_shape=[jax.ShapeDtypeStruct((t, Q_LORA), BF16),
                   jax.ShapeDtypeStruct((t, KV_LORA), BF16),
                   jax.ShapeDtypeStruct((t, HEAD_DIM), F32)],
        compiler_params=_params(1),
        name="mla_latent_norm",
    )(p_lat, g_cq.reshape(1, Q_LORA), g_ckv.reshape(1, KV_LORA))


def _mm_kernel(a_ref, b_ref, o_ref):
    o_ref[...] = jnp.dot(a_ref[...], b_ref[...], preferred_element_type=F32).astype(o_ref.dtype)


def _mm_res_kernel(a_ref, b_ref, r_ref, o_ref):
    o_ref[...] = r_ref[...] + jnp.dot(a_ref[...], b_ref[...], preferred_element_type=F32)


def _mm_res_acc_kernel(a_ref, b_ref, r_ref, o_ref):
    @pl.when(pl.program_id(2) == 0)
    def _():
        o_ref[...] = r_ref[...]

    o_ref[...] += jnp.dot(a_ref[...], b_ref[...], preferred_element_type=F32)


def matmul(a, b, *, out_dtype=F32, residual=None, tm=512, tn=1024):
    m, kd = a.shape
    n = b.shape[1]
    tm, tn = min(tm, m), min(tn, n)
    in_specs = [pl.BlockSpec((tm, kd), lambda j, i: (i, 0)),
                pl.BlockSpec((kd, tn), lambda j, i: (0, j))]
    args = [a, b]
    body = _mm_kernel
    if residual is not None:
        body = _mm_res_kernel
        in_specs.append(pl.BlockSpec((tm, tn), lambda j, i: (i, j)))
        args.append(residual)
    return pl.pallas_call(
        body,
        grid=(n // tn, m // tm),
        in_specs=in_specs,
        out_specs=pl.BlockSpec((tm, tn), lambda j, i: (i, j)),
        out_shape=jax.ShapeDtypeStruct((m, n), out_dtype),
        compiler_params=_params(2),
        name="matmul_res" if residual is not None else "matmul",
    )(*args)


def matmul_res_ktiled(a, b, residual, tm=1024, tn=1024, tk=2048):
    m, kd = a.shape
    n = b.shape[1]
    return pl.pallas_call(
        _mm_res_acc_kernel,
        grid=(m // tm, n // tn, kd // tk),
        in_specs=[pl.BlockSpec((tm, tk), lambda i, j, k: (i, k)),
                  pl.BlockSpec((tk, tn), lambda i, j, k: (k, j)),
                  pl.BlockSpec((tm, tn), lambda i, j, k: (i, j))],
        out_specs=pl.BlockSpec((tm, tn), lambda i, j, k: (i, j)),
        out_shape=jax.ShapeDtypeStruct((m, n), F32),
        compiler_params=_params(3),
        name="matmul_res_ktiled",
    )(a, b, residual)


def _gelu_exact(x):
    return 0.5 * x * (1.0 + lax.erf(x * (1.0 / math.sqrt(2.0))))


def _peer_act_kernel(h_ref, u_ref, g_hbm, o_ref, g_buf, g_sem):
    j, i = pl.program_id(0), pl.program_id(1)
    tm, tn = o_ref.shape
    n_key0 = tn // PEER_NKEYS
    copies = [pltpu.make_async_copy(g_hbm.at[pl.ds(i * tm, tm), j * n_key0 + a, :], g_buf.at[a], g_sem.at[a])
              for a in range(n_key0)]
    for cp in copies:
        cp.start()
    act = _gelu_exact(lax.dot_general(h_ref[...], u_ref[...], _NT, preferred_element_type=F32))
    for a, cp in enumerate(copies):
        cp.wait()
        cols = slice(a * PEER_NKEYS, (a + 1) * PEER_NKEYS)
        o_ref[:, cols] = (act[:, cols] * g_buf[a]).astype(o_ref.dtype)


def peer_gated_act(hn, u, gates, tm=512, tn=1024):
    t, d = hn.shape
    e = u.shape[0]
    tm = min(tm, t)
    n_key0 = tn // PEER_NKEYS
    return pl.pallas_call(
        _peer_act_kernel,
        grid=(e // tn, t // tm),
        in_specs=[pl.BlockSpec((tm, d), lambda j, i: (i, 0)),
                  pl.BlockSpec((tn, d), lambda j, i: (j, 0)),
                  pl.BlockSpec(memory_space=pl.ANY)],
        out_specs=pl.BlockSpec((tm, tn), lambda j, i: (i, j)),
        out_shape=jax.ShapeDtypeStruct((t, e), BF16),
        scratch_shapes=[pltpu.VMEM((n_key0, tm, PEER_NKEYS), F32), pltpu.SemaphoreType.DMA((n_key0,))],
        compiler_params=_params(2),
        name="peer_gated_act",
    )(hn, u, gates)


def _rms_lanes(x, g):
    ms = jnp.mean(x * x, axis=-1, keepdims=True)
    return x * lax.rsqrt(ms + EPS) * g


def _rms_halves(x, g):
    lane = lax.broadcasted_iota(I32, x.shape, 1)
    lo = lane < DIFF_DH
    xx = x * x
    ms_lo = jnp.sum(jnp.where(lo, xx, 0.0), axis=-1, keepdims=True) * (1.0 / DIFF_DH)
    ms_hi = jnp.sum(jnp.where(lo, 0.0, xx), axis=-1, keepdims=True) * (1.0 / DIFF_DH)
    inv = jnp.where(lo, lax.rsqrt(ms_lo + EPS), lax.rsqrt(ms_hi + EPS))
    return x * inv * g


def _exp2_softmax(s2):
    p = jnp.exp2(s2 - jnp.max(s2, axis=-1, keepdims=True))
    return p, jnp.sum(p, axis=-1, keepdims=True)


def _alibi_tables_kernel(slope_ref, alibi_ref, dil_ref, mult_ref, *, s_len):
    shape = alibi_ref.shape
    d = lax.broadcasted_iota(I32, shape, 1) - lax.broadcasted_iota(I32, shape, 0) - (s_len - TAB_ROWS)
    ad = jnp.abs(d)
    alibi = (-LOG2E * slope_ref[pl.program_id(0)]) * ad.astype(F32)
    mult = jnp.zeros(shape, F32)
    for dil in DIL_DILATIONS:
        mult = mult + jnp.where(((d & (dil - 1)) == 0) & (ad <= DIL_HALF * dil), 1.0, 0.0)
    alibi_ref[...] = alibi
    dil_ref[...] = jnp.where(mult > 0.0, alibi, NEG_INF)
    mult_ref[...] = mult


def alibi_tables(s_len):
    slopes = jnp.asarray(2.0 ** (-8.0 * np.arange(1, N_HEADS + 1) / N_HEADS), F32)
    width = 2 * s_len - TAB_ROWS
    per_head = pl.BlockSpec((None, TAB_ROWS, width), lambda h: (h, 0, 0))
    return pl.pallas_call(
        functools.partial(_alibi_tables_kernel, s_len=s_len),
        grid=(N_HEADS,),
        in_specs=[pl.BlockSpec(memory_space=pltpu.SMEM)],
        out_specs=[per_head, per_head, pl.BlockSpec((TAB_ROWS, width), lambda h: (0, 0))],
        out_shape=[jax.ShapeDtypeStruct((N_HEADS, TAB_ROWS, width), F32),
                   jax.ShapeDtypeStruct((N_HEADS, TAB_ROWS, width), F32),
                   jax.ShapeDtypeStruct((TAB_ROWS, width), F32)],
        compiler_params=_params(1),
        name="alibi_tables",
    )(slopes)


def _table_tile(tab_ref, r0, k_lo, k_hi, s_len):
    row = r0 % TAB_ROWS
    lane0 = s_len - TAB_ROWS - (r0 - row) + k_lo
    return tab_ref[row:row + ATT_ROWS, lane0:lane0 + (k_hi - k_lo)]


def _head_specs(s, col0, width=HEAD_DIM):
    return [pl.BlockSpec((None, s, width), lambda h, bi, c=c: (bi, 0, col0 + c * N_HEADS + h)) for c in range(3)]


def _diff_attn_kernel(q_ref, k_ref, v_ref, gq_ref, gk_ref, go_ref, lam_ref, tab_ref, o_ref, kn_ref, vb_ref,
                      *, lambda_init):
    s_len = k_ref.shape[0]
    kn_ref[...] = _rms_halves(k_ref[...], gk_ref[...]).astype(BF16)
    vb_ref[...] = v_ref[...].astype(BF16)
    lv = lam_ref[...]
    lam = (jnp.exp(jnp.sum(lv[0:1] * lv[1:2], axis=-1, keepdims=True))
           - jnp.exp(jnp.sum(lv[2:3] * lv[3:4], axis=-1, keepdims=True)) + lambda_init)
    lo = lax.broadcasted_iota(I32, (ATT_ROWS, HEAD_DIM), 1) < DIFF_DH
    for r0 in range(0, s_len, ATT_ROWS):
        qn = _rms_halves(q_ref[r0:r0 + ATT_ROWS, :], gq_ref[...]) * (DIFF_DH ** -0.5 * LOG2E)
        tab = _table_tile(tab_ref, r0, 0, s_len, s_len)
        p0, l0 = _exp2_softmax(lax.dot_general(jnp.where(lo, qn, 0.0).astype(BF16), kn_ref[...], _NT,
                                               preferred_element_type=F32) + tab)
        p1, l1 = _exp2_softmax(lax.dot_general(jnp.where(lo, 0.0, qn).astype(BF16), kn_ref[...], _NT,
                                               preferred_element_type=F32) + tab)
        w = p0 * (1.0 / l0) - p1 * (lam / l1)
        o = jnp.dot(w.astype(BF16), vb_ref[...], preferred_element_type=F32)
        o_ref[r0:r0 + ATT_ROWS, :] = (_rms_lanes(o, go_ref[...]) * (1.0 - lambda_init)).astype(o_ref.dtype)


def diff_attention(p, col0, lam_vecs, g_qk, g_out, lambda_init, alibi):
    b, s, _ = p.shape
    lam_pad = jnp.pad(lam_vecs, ((0, 0), (0, HEAD_DIM - DIFF_DH)))
    gq = jnp.tile(g_qk[0], 2).reshape(1, HEAD_DIM)
    gk = jnp.tile(g_qk[1], 2).reshape(1, HEAD_DIM)
    vec = pl.BlockSpec((1, HEAD_DIM), lambda h, bi: (0, 0))
    return pl.pallas_call(
        functools.partial(_diff_attn_kernel, lambda_init=lambda_init),
        grid=(N_HEADS, b),
        in_specs=_head_specs(s, col0) + [
            vec, vec, vec,
            pl.BlockSpec((4, HEAD_DIM), lambda h, bi: (0, 0)),
            pl.BlockSpec((None,) + alibi.shape[1:], lambda h, bi: (h, 0, 0))],
        out_specs=pl.BlockSpec((None, s, HEAD_DIM), lambda h, bi: (bi, 0, h)),
        out_shape=jax.ShapeDtypeStruct((b, s, GROUP_WIDTH), BF16),
        scratch_shapes=[pltpu.VMEM((s, HEAD_DIM), BF16), pltpu.VMEM((s, HEAD_DIM), BF16)],
        compiler_params=_params(2),
        name="diff_attention",
    )(p, p, p, gq, gk, g_out.reshape(1, HEAD_DIM), lam_pad, alibi)


def _dilated_attn_kernel(q_ref, k_ref, v_ref, gq_ref, gk_ref, tab_ref, mult_ref, o_ref, kn_ref, vb_ref):
    s_len = k_ref.shape[0]
    kn_ref[...] = _rms_lanes(k_ref[...], gk_ref[...]).astype(BF16)
    vb_ref[...] = v_ref[...].astype(BF16)
    for r0 in range(0, s_len, ATT_ROWS):
        k_lo, k_hi = max(0, r0 - DIL_REACH), min(s_len, r0 + ATT_ROWS + DIL_REACH)
        qn = (_rms_lanes(q_ref[r0:r0 + ATT_ROWS, :], gq_ref[...]) * (HEAD_DIM ** -0.5 * LOG2E)).astype(BF16)
        s2 = (lax.dot_general(qn, kn_ref[k_lo:k_hi, :], _NT, preferred_element_type=F32)
              + _table_tile(tab_ref, r0, k_lo, k_hi, s_len))
        p = jnp.exp2(s2 - jnp.max(s2, axis=-1, keepdims=True)) * _table_tile(mult_ref, r0, k_lo, k_hi, s_len)
        l = jnp.sum(p, axis=-1, keepdims=True)
        o = jnp.dot(p.astype(BF16), vb_ref[k_lo:k_hi, :], preferred_element_type=F32)
        o_ref[r0:r0 + ATT_ROWS, :] = (o * (1.0 / l)).astype(o_ref.dtype)


def dilated_attention(p, col0, g_qk, dil_tab, mult_tab):
    b, s, _ = p.shape
    vec = pl.BlockSpec((1, HEAD_DIM), lambda h, bi: (0, 0))
    return pl.pallas_call(
        _dilated_attn_kernel,
        grid=(N_HEADS, b),
        in_specs=_head_specs(s, col0) + [
            vec, vec,
            pl.BlockSpec((None,) + dil_tab.shape[1:], lambda h, bi: (h, 0, 0)),
            pl.BlockSpec(mult_tab.shape, lambda h, bi: (0, 0))],
        out_specs=pl.BlockSpec((None, s, HEAD_DIM), lambda h, bi: (bi, 0, h)),
        out_shape=jax.ShapeDtypeStruct((b, s, GROUP_WIDTH), BF16),
        scratch_shapes=[pltpu.VMEM((s, HEAD_DIM), BF16), pltpu.VMEM((s, HEAD_DIM), BF16)],
        compiler_params=_params(2),
        name="dilated_attention",
    )(p, p, p, g_qk[0].reshape(1, HEAD_DIM), g_qk[1].reshape(1, HEAD_DIM), dil_tab, mult_tab)


def _na_window_start(q_row0, n_rows):
    return min(max(q_row0 - NA_ROWS // 2, 0), n_rows - NA_K_ROWS)


def _na_attn_kernel(q_ref, k_ref, v_ref, gq_ref, gk_ref, bias_ref, o_ref, kn_ref, vb_ref):
    n_rows = k_ref.shape[0] // GRID_W
    kn_ref[...] = _rms_lanes(k_ref[...], gk_ref[...]).astype(BF16)
    vb_ref[...] = v_ref[...].astype(BF16)
    tq, tk = NA_Q_ROWS * GRID_W, NA_K_ROWS * GRID_W
    for qb in range(n_rows // NA_Q_ROWS):
        k0 = _na_window_start(qb * NA_Q_ROWS, n_rows) * GRID_W
        qn = (_rms_lanes(q_ref[qb * tq:(qb + 1) * tq, :], gq_ref[...]) * (HEAD_DIM ** -0.5 * LOG2E)).astype(BF16)
        p, l = _exp2_softmax(lax.dot_general(qn, kn_ref[k0:k0 + tk, :], _NT, preferred_element_type=F32)
                             + bias_ref[qb])
        o = jnp.dot(p.astype(BF16), vb_ref[k0:k0 + tk, :], preferred_element_type=F32)
        o_ref[qb * tq:(qb + 1) * tq, :] = (o * (1.0 / l)).astype(o_ref.dtype)


def _na_bias_kernel(t_ref, o_ref, *, n_rows):
    lane = lax.broadcasted_iota(I32, (GRID_W, 2 * GRID_W), 1)
    for qb in range(n_rows // NA_Q_ROWS):
        q_row0 = qb * NA_Q_ROWS
        k_row0 = _na_window_start(q_row0, n_rows)
        for a in range(NA_Q_ROWS):
            rq = q_row0 + a
            rs = min(max(rq - NA_ROWS // 2, 0), n_rows - NA_ROWS)
            for wp in range(NA_K_ROWS // 2):
                rk = k_row0 + 2 * wp
                ok0, ok1 = rs <= rk < rs + NA_ROWS, rs <= rk + 1 < rs + NA_ROWS
                m = min(max(rk - rq + NA_ROWS, 0), 2 * NA_ROWS - 1)
                tile = t_ref[m] * LOG2E
                if not (ok0 and ok1):
                    keep = (lane < GRID_W) if ok0 else (lane >= GRID_W)
                    tile = jnp.where(keep, tile, NEG_INF) if (ok0 or ok1) else jnp.full_like(tile, NEG_INF)
                o_ref[qb, a * GRID_W:(a + 1) * GRID_W, wp * 2 * GRID_W:(wp + 1) * 2 * GRID_W] = tile


def _na_bias(rpb, n_rows):
    n_rel_r = 2 * NA_ROWS - 1
    pad = GRID_W - NA_COLS
    rp = jnp.pad(rpb, ((0, 0), (0, 0), (pad, pad)))
    toe = jnp.stack([rp[:, :, NA_COLS - 1 + pad - cq:NA_COLS - 1 + pad - cq + GRID_W] for cq in range(GRID_W)],
                    axis=2)
    cq = np.arange(GRID_W)
    cs = np.clip(cq - NA_COLS // 2, 0, GRID_W - NA_COLS)
    col_ok = (cq[None, :] >= cs[:, None]) & (cq[None, :] < cs[:, None] + NA_COLS)
    toe = jnp.where(col_ok[None, None], toe, NEG_INF)
    ext = jnp.pad(toe, ((0, 0), (1, 1), (0, 0), (0, 0)), constant_values=NEG_INF)
    pairs = jnp.concatenate([ext[:, :n_rel_r + 1], ext[:, 1:]], axis=-1)
    tq, tk = NA_Q_ROWS * GRID_W, NA_K_ROWS * GRID_W
    nqb = n_rows // NA_Q_ROWS
    return pl.pallas_call(
        functools.partial(_na_bias_kernel, n_rows=n_rows),
        grid=(N_HEADS,),
        in_specs=[pl.BlockSpec((None, n_rel_r + 1, GRID_W, 2 * GRID_W), lambda h: (h, 0, 0, 0))],
        out_specs=pl.BlockSpec((None, nqb, tq, tk), lambda h: (h, 0, 0, 0)),
        out_shape=jax.ShapeDtypeStruct((N_HEADS, nqb, tq, tk), F32),
        compiler_params=_params(1),
        name="na_bias",
    )(pairs)


def neighbourhood_attention(p, col0, g_qk, rpb):
    b, s, _ = p.shape
    bias = _na_bias(rpb, s // GRID_W)
    vec = pl.BlockSpec((1, HEAD_DIM), lambda h, bi: (0, 0))
    return pl.pallas_call(
        _na_attn_kernel,
        grid=(N_HEADS, b),
        in_specs=_head_specs(s, col0) + [
            vec, vec,
            pl.BlockSpec((None,) + bias.shape[1:], lambda h, bi: (h, 0, 0, 0))],
        out_specs=pl.BlockSpec((None, s, HEAD_DIM), lambda h, bi: (bi, 0, h)),
        out_shape=jax.ShapeDtypeStruct((b, s, GROUP_WIDTH), BF16),
        scratch_shapes=[pltpu.VMEM((s, HEAD_DIM), BF16), pltpu.VMEM((s, HEAD_DIM), BF16)],
        compiler_params=_params(2),
        name="neighbourhood_attention",
    )(p, p, p, g_qk[0].reshape(1, HEAD_DIM), g_qk[1].reshape(1, HEAD_DIM), bias)


def _mla_norm_rope(x, g, cos, sin):
    ms = jnp.sum(x * x, axis=-1, keepdims=True) * (1.0 / MLA_QK)
    xn = x * lax.rsqrt(ms + EPS) * g
    xr = xn[:, HEAD_DIM:]
    partner = pltpu.roll(xr, QK_ROPE // 2, 1) + pltpu.roll(xr, HEAD_DIM - QK_ROPE // 2, 1)
    return jnp.concatenate([xn[:, :HEAD_DIM], xr * cos + partner * sin], axis=-1)


def _mla_attn_kernel(q_ref, kn_in_ref, v_ref, kr_ref, cos_ref, sin_ref, gq_ref, gk_ref, o_ref, kf_ref, vb_ref):
    s_len = v_ref.shape[0]
    k = jnp.concatenate([kn_in_ref[...], kr_ref[...]], axis=-1)
    kf_ref[...] = _mla_norm_rope(k, gk_ref[...], cos_ref[...], sin_ref[...]).astype(BF16)
    vb_ref[...] = v_ref[...].astype(BF16)
    for r0 in range(0, s_len, ATT_ROWS):
        rows = slice(r0, r0 + ATT_ROWS)
        qf = (_mla_norm_rope(q_ref[rows, :], gq_ref[...], cos_ref[rows, :], sin_ref[rows, :])
              * (MLA_QK ** -0.5 * LOG2E)).astype(BF16)
        p, l = _exp2_softmax(lax.dot_general(qf, kf_ref[...], _NT, preferred_element_type=F32))
        o = jnp.dot(p.astype(BF16), vb_ref[...], preferred_element_type=F32)
        o_ref[rows, :] = (o * (1.0 / l)).astype(o_ref.dtype)


def mla_attention(q_up, kv_up, k_rope, g_qk, cos_t, sin_t):
    b, s, _ = q_up.shape
    pad = MLA_QK_PAD - MLA_QK
    gq = jnp.pad(g_qk[0], (0, pad)).reshape(1, MLA_QK_PAD)
    gk = jnp.pad(g_qk[1], (0, pad)).reshape(1, MLA_QK_PAD)
    vec = pl.BlockSpec((1, MLA_QK_PAD), lambda bi, h: (0, 0))
    rope = pl.BlockSpec((s, HEAD_DIM), lambda bi, h: (0, 0))
    return pl.pallas_call(
        _mla_attn_kernel,
        grid=(b, N_HEADS),
        in_specs=[pl.BlockSpec((None, s, MLA_QK_PAD), lambda bi, h: (bi, 0, h)),
                  pl.BlockSpec((None, s, HEAD_DIM), lambda bi, h: (bi, 0, 2 * h)),
                  pl.BlockSpec((None, s, HEAD_DIM), lambda bi, h: (bi, 0, 2 * h + 1)),
                  pl.BlockSpec((None, s, HEAD_DIM), lambda bi, h: (bi, 0, 0)),
                  rope, rope, vec, vec],
        out_specs=pl.BlockSpec((None, s, HEAD_DIM), lambda bi, h: (bi, 0, h)),
        out_shape=jax.ShapeDtypeStruct((b, s, GROUP_WIDTH), BF16),
        scratch_shapes=[pltpu.VMEM((s, MLA_QK_PAD), BF16), pltpu.VMEM((s, HEAD_DIM), BF16)],
        compiler_params=_params(2),
        name="mla_attention",
    )(q_up, kv_up, kv_up, k_rope, cos_t, sin_t, gq, gk)


def _top_k_rows(s, order, payload=None):
    n, tt = s.shape
    out_row = lax.broadcasted_iota(I32, (PEER_TOPK, tt), 0)
    vals = jnp.zeros((PEER_TOPK, tt), F32)
    picks = jnp.zeros((PEER_TOPK, tt), I32)
    for r in range(PEER_TOPK):
        m = jnp.max(s, axis=0, keepdims=True)
        first = jnp.min(jnp.where(s == m, order, jnp.iinfo(jnp.int32).max), axis=0, keepdims=True)
        sel = order == first
        vals = jnp.where(out_row == r, m, vals)
        pick = first if payload is None else jnp.max(jnp.where(sel, payload, 0), axis=0, keepdims=True)
        picks = jnp.where(out_row == r, pick, picks)
        s = jnp.where(sel, -jnp.inf, s)
    return vals, picks


_PAIR_BLOCKS = (("i", 16, 0), ("i", 8, 1), ("i", 8, 2), ("i", 8, 3), ("j", 16, 0), ("j", 8, 1), ("j", 8, 2))
_PAIR_SPLIT = 4


def _pair_candidates(s0, i0, s1, i1):
    tt = s0.shape[1]
    scores, experts, flats = [], [], []
    for vary, n, fixed in _PAIR_BLOCKS:
        r = lax.broadcasted_iota(I32, (n, tt), 0)
        limit = PEER_TOPK // (fixed + 1)
        if vary == "i":
            sc = s0[:n] + s1[fixed:fixed + 1]
            ex = i0[:n] * PEER_NKEYS + i1[fixed:fixed + 1]
            flat = r * PEER_TOPK + fixed
            valid = r < limit
        else:
            sc = s0[fixed:fixed + 1] + s1[:n]
            ex = i0[fixed:fixed + 1] * PEER_NKEYS + i1[:n]
            flat = fixed * PEER_TOPK + r
            valid = (r < limit) & (r >= _PAIR_SPLIT)
        scores.append(jnp.where(valid, sc, -jnp.inf))
        experts.append(ex)
        flats.append(flat)
    return jnp.concatenate(scores, 0), jnp.concatenate(experts, 0), jnp.concatenate(flats, 0)


def _peer_route_kernel(q_ref, keys_ref, expert_ref, gate_ref):
    tt = q_ref.shape[0]
    lanes = HEAD_DIM
    key_row = lax.broadcasted_iota(I32, (PEER_NKEYS, lanes), 0)
    for c0 in range(0, tt, lanes):
        tops = []
        for c in range(2):
            qc = q_ref[c0:c0 + lanes, c * HEAD_DIM:(c + 1) * HEAD_DIM].astype(BF16)
            sc = lax.dot_general(keys_ref[c], qc, _NT, preferred_element_type=F32)
            tops.append(_top_k_rows(sc, key_row))
        (s0, i0), (s1, i1) = tops
        cand_s, cand_e, cand_flat = _pair_candidates(s0, i0, s1, i1)
        best_s, best_e = _top_k_rows(cand_s, cand_flat, cand_e)
        e = jnp.exp(best_s - jnp.max(best_s, axis=0, keepdims=True))
        gate_ref[:, c0:c0 + lanes] = e * (1.0 / jnp.sum(e, axis=0, keepdims=True))
        expert_ref[:, c0:c0 + lanes] = best_e


def peer_route(q, sub_keys, tt=256):
    t = q.shape[0]
    n_sel = PEER_HEADS * PEER_TOPK
    return pl.pallas_call(
        _peer_route_kernel,
        grid=(t // tt, PEER_HEADS),
        in_specs=[pl.BlockSpec((tt, 2 * HEAD_DIM), lambda i, h: (i, h)),
                  pl.BlockSpec((2, PEER_NKEYS, HEAD_DIM), lambda i, h: (h, 0, 0))],
        out_specs=[pl.BlockSpec((PEER_TOPK, tt), lambda i, h: (h, i)),
                   pl.BlockSpec((PEER_TOPK, tt), lambda i, h: (h, i))],
        out_shape=[jax.ShapeDtypeStruct((n_sel, t), I32), jax.ShapeDtypeStruct((n_sel, t), F32)],
        compiler_params=_params(2),
        name="peer_route",
    )(q, sub_keys)


def _peer_gate_matrix_kernel(expert_ref, gate_ref, o_ref, e_s, g_s):
    e_s[...] = expert_ref[...].T
    g_s[...] = gate_ref[...].T
    n_sel = e_s.shape[1]
    row = lax.broadcasted_iota(I32, (PEER_NKEYS, n_sel), 0)

    def body(t, carry):
        e_row = e_s[pl.ds(t, 1), :]
        g_row = g_s[pl.ds(t, 1), :]
        hi = jnp.where(row == (e_row >> 7), g_row, 0.0).astype(BF16)
        lo = jnp.where(row == (e_row & (PEER_NKEYS - 1)), 1.0, 0.0).astype(BF16)
        o_ref[t] = lax.dot_general(hi, lo, _NT, preferred_element_type=F32)
        return carry

    lax.fori_loop(0, e_s.shape[0], body, 0, unroll=8)


def peer_gate_matrix(expert_t, gate_t, tg=128):
    n_sel, t = expert_t.shape
    return pl.pallas_call(
        _peer_gate_matrix_kernel,
        grid=(t // tg,),
        in_specs=[pl.BlockSpec((n_sel, tg), lambda i: (0, i)), pl.BlockSpec((n_sel, tg), lambda i: (0, i))],
        out_specs=pl.BlockSpec((tg, PEER_NKEYS, PEER_NKEYS), lambda i: (i, 0, 0)),
        out_shape=jax.ShapeDtypeStruct((t, PEER_NKEYS, PEER_NKEYS), F32),
        scratch_shapes=[pltpu.VMEM((tg, n_sel), I32), pltpu.VMEM((tg, n_sel), F32)],
        compiler_params=_params(1),
        name="peer_gate_matrix",
    )(expert_t, gate_t)


def _rope_tables(s):
    inv_freq = 1.0 / (ROPE_THETA ** (np.arange(0, QK_ROPE, 2, dtype=np.float32) / QK_ROPE))
    ang = jnp.arange(s, dtype=F32)[:, None] * jnp.asarray(inv_freq, F32)[None, :]
    cos, sin = jnp.cos(ang), jnp.sin(ang)
    zeros = jnp.zeros((s, HEAD_DIM - QK_ROPE), F32)
    return (jnp.concatenate([cos, cos, zeros], axis=-1), jnp.concatenate([-sin, sin, zeros], axis=-1))


def _layer(x2, b, s, l, tables, g_mix, w_in, w_out, diff_lambda, diff_g_qk, diff_g_out, dil_g_qk, na_g_qk,
           na_rpb, mla_g_cq, mla_g_ckv, mla_w_uq, mla_w_ukv, mla_g_qk, g_ffn, peer_w_q, peer_sub_keys, peer_u,
           peer_v):
    t = b * s
    cos_t, sin_t, alibi, dil_tab, mult_tab = tables
    lambda_init = 0.8 - 0.6 * math.exp(-0.3 * l)
    n_qkv = 3 * QKV_COLS

    h = rms_rows(x2, g_mix)
    p = matmul(h, w_in[:, :n_qkv].astype(BF16)).reshape(b, s, n_qkv)
    w_lat = jnp.pad(w_in[:, n_qkv:], ((0, 0), (0, HEAD_DIM - QK_ROPE))).astype(BF16)
    p_lat = matmul(h, w_lat, tn=MLA_LATENT_PAD)
    c_q, c_kv, k_rope = mla_latent_norm(p_lat, mla_g_cq, mla_g_ckv)

    blocks = QKV_COLS // HEAD_DIM
    o_a = diff_attention(p, 0, diff_lambda, diff_g_qk, diff_g_out, lambda_init, alibi)
    o_b = dilated_attention(p, blocks, dil_g_qk, dil_tab, mult_tab)
    o_c = neighbourhood_attention(p, 2 * blocks, na_g_qk, na_rpb)

    w_uq = jnp.pad(mla_w_uq.reshape(Q_LORA, N_HEADS, MLA_QK), ((0, 0), (0, 0), (0, MLA_QK_PAD - MLA_QK)))
    q_up = matmul(c_q, w_uq.reshape(Q_LORA, N_HEADS * MLA_QK_PAD).astype(BF16), tm=1024)
    kv_up = matmul(c_kv, mla_w_ukv.astype(BF16), tm=1024)
    o_d = mla_attention(q_up.reshape(b, s, -1), kv_up.reshape(b, s, -1), k_rope.reshape(b, s, HEAD_DIM),
                        mla_g_qk, cos_t, sin_t)

    mixed = jnp.concatenate([o_a, o_b, o_c, o_d], axis=-1).reshape(t, 4 * GROUP_WIDTH)
    x2 = matmul(mixed, w_out.astype(BF16), residual=x2)

    hf = rms_rows(x2, g_ffn)
    q = matmul(hf, peer_w_q.astype(BF16))
    keys = peer_sub_keys.reshape(2 * PEER_HEADS, PEER_NKEYS, HEAD_DIM).astype(BF16)
    expert_t, gate_t = peer_route(q, keys)
    gates = peer_gate_matrix(expert_t, gate_t)
    act = peer_gated_act(hf, peer_u.astype(BF16), gates)
    return matmul_res_ktiled(act, peer_v.astype(BF16), x2)


def kernel(x, g_mix, w_in, w_out, diff_lambda, diff_g_qk, diff_g_out, dil_g_qk, na_g_qk, na_rpb, mla_g_cq,
           mla_g_ckv, mla_w_uq, mla_w_ukv, mla_g_qk, g_ffn, peer_w_q, peer_sub_keys, peer_u, peer_v):
    b, s, d = x.shape
    depth = g_mix.shape[0]
    tables = _rope_tables(s) + tuple(alibi_tables(s))
    x2 = x.reshape(b * s, d)
    per_layer = (g_mix, w_in, w_out, diff_lambda, diff_g_qk, diff_g_out, dil_g_qk, na_g_qk, na_rpb, mla_g_cq,
                 mla_g_ckv, mla_w_uq, mla_w_ukv, mla_g_qk, g_ffn, peer_w_q, peer_sub_keys, peer_u, peer_v)
    for l in range(depth):
        x2 = _layer(x2, b, s, l, tables, *(w[l] for w in per_layer))
    return x2.reshape(b, s, d)
```

```python
import functools
import math

import jax
import jax.numpy as jnp
import numpy as np
from jax import lax
from jax.experimental import pallas as pl
from jax.experimental.pallas import tpu as pltpu

F32 = jnp.float32
BF16 = jnp.bfloat16
I32 = jnp.int32

D_MODEL = 4096
HEAD_DIM = 128
N_HEADS = 8
GROUP_WIDTH = N_HEADS * HEAD_DIM
QKV_COLS = 3 * GROUP_WIDTH
EPS = 1e-6
NEG_INF = -1e30
LOG2E = math.log2(math.e)
DIFF_DH = HEAD_DIM // 2
DIL_HALF = 64
DIL_DILATIONS = (1, 4, 16)
DIL_REACH = DIL_HALF * max(DIL_DILATIONS)
GRID_W = 64
NA_ROWS = 8
NA_COLS = 16
NA_Q_ROWS = 4
NA_K_ROWS = NA_Q_ROWS + NA_ROWS
Q_LORA = 768
KV_LORA = 512
QK_NOPE = 128
QK_ROPE = 64
MLA_QK = QK_NOPE + QK_ROPE
MLA_QK_PAD = 2 * HEAD_DIM
ROPE_THETA = 10000.0
PEER_HEADS = 8
PEER_NKEYS = 128
PEER_TOPK = 16
PEER_EXPERTS = PEER_NKEYS * PEER_NKEYS
MLA_LATENT_PAD = Q_LORA + KV_LORA + HEAD_DIM
ATT_ROWS = 256
TAB_ROWS = 512

VMEM_LIMIT = 48 * 1024 * 1024

_NT = (((1,), (1,)), ((), ()))


def _params(grid_rank):
    return pltpu.CompilerParams(dimension_semantics=("arbitrary",) * grid_rank, vmem_limit_bytes=VMEM_LIMIT)


def _rms_rows_kernel(x_ref, g_ref, o_ref):
    x = x_ref[...]
    ms = jnp.mean(x * x, axis=-1, keepdims=True)
    o_ref[...] = (x * lax.rsqrt(ms + EPS) * g_ref[...]).astype(o_ref.dtype)


def rms_rows(x, g, tm=256):
    t, c = x.shape
    return pl.pallas_call(
        _rms_rows_kernel,
        grid=(t // tm,),
        in_specs=[pl.BlockSpec((tm, c), lambda i: (i, 0)), pl.BlockSpec((1, c), lambda i: (0, 0))],
        out_specs=pl.BlockSpec((tm, c), lambda i: (i, 0)),
        out_shape=jax.ShapeDtypeStruct((t, c), BF16),
        compiler_params=_params(1),
        name="rms_rows",
    )(x, g.reshape(1, c))


def _mla_latent_kernel(p_ref, gq_ref, gkv_ref, cq_ref, ckv_ref, kr_ref):
    cq = p_ref[:, :Q_LORA]
    ckv = p_ref[:, Q_LORA:Q_LORA + KV_LORA]
    cq_ref[...] = (cq * lax.rsqrt(jnp.mean(cq * cq, axis=-1, keepdims=True) + EPS) * gq_ref[...]).astype(BF16)
    ckv_ref[...] = (ckv * lax.rsqrt(jnp.mean(ckv * ckv, axis=-1, keepdims=True) + EPS) * gkv_ref[...]).astype(BF16)
    kr_ref[...] = p_ref[:, Q_LORA + KV_LORA:]


def mla_latent_norm(p_lat, g_cq, g_ckv, tm=512):
    t = p_lat.shape[0]
    return pl.pallas_call(
        _mla_latent_kernel,
        grid=(t // tm,),
        in_specs=[pl.BlockSpec((tm, MLA_LATENT_PAD), lambda i: (i, 0)),
                  pl.BlockSpec((1, Q_LORA), lambda i: (0, 0)),
                  pl.BlockSpec((1, KV_LORA), lambda i: (0, 0))],
        out_specs=[pl.BlockSpec((tm, Q_LORA), lambda i: (i, 0)),
                   pl.BlockSpec((tm, KV_LORA), lambda i: (i, 0)),
                   pl.BlockSpec((tm, HEAD_DIM), lambda i: (i, 0))],
        out_shape=[jax.ShapeDtypeStruct((t, Q_LORA), BF16),
                   jax.ShapeDtypeStruct((t, KV_LORA), BF16),
                   jax.ShapeDtypeStruct((t, HEAD_DIM), F32)],
        compiler_params=_params(1),
        name="mla_latent_norm",
    )(p_lat, g_cq.reshape(1, Q_LORA), g_ckv.reshape(1, KV_LORA))


def _mm_kernel(a_ref, b_ref, o_ref):
    o_ref[...] = jnp.dot(a_ref[...], b_ref[...], preferred_element_type=F32).astype(o_ref.dtype)


def _mm_res_kernel(a_ref, b_ref, r_ref, o_ref):
    o_ref[...] = r_ref[...] + jnp.dot(a_ref[...], b_ref[...], preferred_element_type=F32)


def _mm_res_acc_kernel(a_ref, b_ref, r_ref, o_ref):
    @pl.when(pl.program_id(2) == 0)
    def _():
        o_ref[...] = r_ref[...]

    o_ref[...] += jnp.dot(a_ref[...], b_ref[...], preferred_element_type=F32)


def matmul(a, b, *, out_dtype=F32, residual=None, tm=512, tn=1024):
    m, kd = a.shape
    n = b.shape[1]
    tm, tn = min(tm, m), min(tn, n)
    in_specs = [pl.BlockSpec((tm, kd), lambda j, i: (i, 0)),
                pl.BlockSpec((kd, tn), lambda j, i: (0, j))]
    args = [a, b]
    body = _mm_kernel
    if residual is not None:
        body = _mm_res_kernel
        in_specs.append(pl.BlockSpec((tm, tn), lambda j, i: (i, j)))
        args.append(residual)
    return pl.pallas_call(
        body,
        grid=(n // tn, m // tm),
        in_specs=in_specs,
        out_specs=pl.BlockSpec((tm, tn), lambda j, i: (i, j)),
        out_shape=jax.ShapeDtypeStruct((m, n), out_dtype),
        compiler_params=_params(2),
        name="matmul_res" if residual is not None else "matmul",
    )(*args)


def matmul_res_ktiled(a, b, residual, tm=1024, tn=1024, tk=2048):
    m, kd = a.shape
    n = b.shape[1]
    return pl.pallas_call(
        _mm_res_acc_kernel,
        grid=(m // tm, n // tn, kd // tk),
        in_specs=[pl.BlockSpec((tm, tk), lambda i, j, k: (i, k)),
                  pl.BlockSpec((tk, tn), lambda i, j, k: (k, j)),
                  pl.BlockSpec((tm, tn), lambda i, j, k: (i, j))],
        out_specs=pl.BlockSpec((tm, tn), lambda i, j, k: (i, j)),
        out_shape=jax.ShapeDtypeStruct((m, n), F32),
        compiler_params=_params(3),
        name="matmul_res_ktiled",
    )(a, b, residual)


def _gelu_exact(x):
    return 0.5 * x * (1.0 + lax.erf(x * (1.0 / math.sqrt(2.0))))


def _peer_act_kernel(h_ref, u_ref, g_ref, o_ref):
    tm, n_key0, n_key1 = g_ref.shape
    sub = 8
    g = jnp.swapaxes(g_ref[...].reshape(tm // sub, sub, n_key0, n_key1), 1, 2)
    act = _gelu_exact(lax.dot_general(h_ref[...], u_ref[...], _NT, preferred_element_type=F32))
    for a in range(n_key0):
        cols = slice(a * n_key1, (a + 1) * n_key1)
        o_ref[:, cols] = (act[:, cols] * g[:, a].reshape(tm, n_key1)).astype(o_ref.dtype)


def peer_gated_act(hn, u, gates, tm=512, tn=1024):
    t, d = hn.shape
    e = u.shape[0]
    tm = min(tm, t)
    n_key0 = tn // PEER_NKEYS
    return pl.pallas_call(
        _peer_act_kernel,
        grid=(e // tn, t // tm),
        in_specs=[pl.BlockSpec((tm, d), lambda j, i: (i, 0)),
                  pl.BlockSpec((tn, d), lambda j, i: (j, 0)),
                  pl.BlockSpec((tm, n_key0, PEER_NKEYS), lambda j, i: (i, j, 0))],
        out_specs=pl.BlockSpec((tm, tn), lambda j, i: (i, j)),
        out_shape=jax.ShapeDtypeStruct((t, e), BF16),
        compiler_params=_params(2),
        name="peer_gated_act",
    )(hn, u, gates)


def _rms_lanes(x, g):
    ms = jnp.mean(x * x, axis=-1, keepdims=True)
    return x * lax.rsqrt(ms + EPS) * g


def _rms_halves(x, g):
    lane = lax.broadcasted_iota(I32, x.shape, 1)
    lo = lane < DIFF_DH
    xx = x * x
    ms_lo = jnp.sum(jnp.where(lo, xx, 0.0), axis=-1, keepdims=True) * (1.0 / DIFF_DH)
    ms_hi = jnp.sum(jnp.where(lo, 0.0, xx), axis=-1, keepdims=True) * (1.0 / DIFF_DH)
    inv = jnp.where(lo, lax.rsqrt(ms_lo + EPS), lax.rsqrt(ms_hi + EPS))
    return x * inv * g


def _exp2_weights(s2):
    return jnp.exp2(s2 - jnp.max(s2, axis=-1, keepdims=True))


def _values_with_ones(v):
    ones_col = jnp.where(lax.broadcasted_iota(I32, v.shape, 1) == 0, 1.0, 0.0)
    return jnp.concatenate([v, ones_col], axis=-1).astype(BF16)


def _weighted_values(p, vx):
    ov = jnp.dot(p.astype(BF16), vx, preferred_element_type=F32)
    return ov[:, :HEAD_DIM], ov[:, HEAD_DIM:HEAD_DIM + 1]


def _alibi_tables_kernel(slope_ref, alibi_ref, dil_ref, mult_ref, *, s_len):
    shape = alibi_ref.shape
    d = lax.broadcasted_iota(I32, shape, 1) - lax.broadcasted_iota(I32, shape, 0) - (s_len - TAB_ROWS)
    ad = jnp.abs(d)
    alibi = (-LOG2E * slope_ref[pl.program_id(0)]) * ad.astype(F32)
    mult = jnp.zeros(shape, F32)
    for dil in DIL_DILATIONS:
        mult = mult + jnp.where(((d & (dil - 1)) == 0) & (ad <= DIL_HALF * dil), 1.0, 0.0)
    alibi_ref[...] = alibi
    dil_ref[...] = jnp.where(mult > 0.0, alibi, NEG_INF)
    mult_ref[...] = mult


def alibi_tables(s_len):
    slopes = jnp.asarray(2.0 ** (-8.0 * np.arange(1, N_HEADS + 1) / N_HEADS), F32)
    width = 2 * s_len - TAB_ROWS
    per_head = pl.BlockSpec((None, TAB_ROWS, width), lambda h: (h, 0, 0))
    return pl.pallas_call(
        functools.partial(_alibi_tables_kernel, s_len=s_len),
        grid=(N_HEADS,),
        in_specs=[pl.BlockSpec(memory_space=pltpu.SMEM)],
        out_specs=[per_head, per_head, pl.BlockSpec((TAB_ROWS, width), lambda h: (0, 0))],
        out_shape=[jax.ShapeDtypeStruct((N_HEADS, TAB_ROWS, width), F32),
                   jax.ShapeDtypeStruct((N_HEADS, TAB_ROWS, width), F32),
                   jax.ShapeDtypeStruct((TAB_ROWS, width), F32)],
        compiler_params=_params(1),
        name="alibi_tables",
    )(slopes)


def _table_tile(tab_ref, r0, k_lo, k_hi, s_len):
    row = r0 % TAB_ROWS
    lane0 = s_len - TAB_ROWS - (r0 - row) + k_lo
    return tab_ref[row:row + ATT_ROWS, lane0:lane0 + (k_hi - k_lo)]


def _head_specs(s, col0, width=HEAD_DIM):
    return [pl.BlockSpec((None, s, width), lambda h, bi, c=c: (bi, 0, col0 + c * N_HEADS + h)) for c in range(3)]


def _diff_attn_kernel(q_ref, k_ref, v_ref, gq_ref, gk_ref, go_ref, lam_ref, tab_ref, o_ref, kn_ref, vb_ref,
                      *, lambda_init):
    s_len = k_ref.shape[0]
    kn_ref[...] = _rms_halves(k_ref[...], gk_ref[...]).astype(BF16)
    vb_ref[...] = _values_with_ones(v_ref[...])
    lv = lam_ref[...]
    lam = (jnp.exp(jnp.sum(lv[0:1] * lv[1:2], axis=-1, keepdims=True))
           - jnp.exp(jnp.sum(lv[2:3] * lv[3:4], axis=-1, keepdims=True)) + lambda_init)
    lo = lax.broadcasted_iota(I32, (ATT_ROWS, HEAD_DIM), 1) < DIFF_DH
    for r0 in range(0, s_len, ATT_ROWS):
        qn = _rms_halves(q_ref[r0:r0 + ATT_ROWS, :], gq_ref[...]) * (DIFF_DH ** -0.5 * LOG2E)
        tab = _table_tile(tab_ref, r0, 0, s_len, s_len)
        p0 = _exp2_weights(lax.dot_general(jnp.where(lo, qn, 0.0).astype(BF16), kn_ref[...], _NT,
                                           preferred_element_type=F32) + tab)
        p1 = _exp2_weights(lax.dot_general(jnp.where(lo, 0.0, qn).astype(BF16), kn_ref[...], _NT,
                                           preferred_element_type=F32) + tab)
        o0, l0 = _weighted_values(p0, vb_ref[...])
        o1, l1 = _weighted_values(p1, vb_ref[...])
        o = o0 * (1.0 / l0) - o1 * (lam / l1)
        o_ref[r0:r0 + ATT_ROWS, :] = (_rms_lanes(o, go_ref[...]) * (1.0 - lambda_init)).astype(o_ref.dtype)


def diff_attention(p, col0, lam_vecs, g_qk, g_out, lambda_init, alibi):
    b, s, _ = p.shape
    lam_pad = jnp.pad(lam_vecs, ((0, 0), (0, HEAD_DIM - DIFF_DH)))
    gq = jnp.tile(g_qk[0], 2).reshape(1, HEAD_DIM)
    gk = jnp.tile(g_qk[1], 2).reshape(1, HEAD_DIM)
    vec = pl.BlockSpec((1, HEAD_DIM), lambda h, bi: (0, 0))
    return pl.pallas_call(
        functools.partial(_diff_attn_kernel, lambda_init=lambda_init),
        grid=(N_HEADS, b),
        in_specs=_head_specs(s, col0) + [
            vec, vec, vec,
            pl.BlockSpec((4, HEAD_DIM), lambda h, bi: (0, 0)),
            pl.BlockSpec((None,) + alibi.shape[1:], lambda h, bi: (h, 0, 0))],
        out_specs=pl.BlockSpec((None, s, HEAD_DIM), lambda h, bi: (bi, 0, h)),
        out_shape=jax.ShapeDtypeStruct((b, s, GROUP_WIDTH), BF16),
        scratch_shapes=[pltpu.VMEM((s, HEAD_DIM), BF16), pltpu.VMEM((s, 2 * HEAD_DIM), BF16)],
        compiler_params=_params(2),
        name="diff_attention",
    )(p, p, p, gq, gk, g_out.reshape(1, HEAD_DIM), lam_pad, alibi)


def _dilated_attn_kernel(q_ref, k_ref, v_ref, gq_ref, gk_ref, tab_ref, mult_ref, o_ref, kn_ref, vb_ref):
    s_len = k_ref.shape[0]
    kn_ref[...] = _rms_lanes(k_ref[...], gk_ref[...]).astype(BF16)
    vb_ref[...] = _values_with_ones(v_ref[...])
    for r0 in range(0, s_len, ATT_ROWS):
        k_lo, k_hi = max(0, r0 - DIL_REACH), min(s_len, r0 + ATT_ROWS + DIL_REACH)
        qn = (_rms_lanes(q_ref[r0:r0 + ATT_ROWS, :], gq_ref[...]) * (HEAD_DIM ** -0.5 * LOG2E)).astype(BF16)
        s2 = (lax.dot_general(qn, kn_ref[k_lo:k_hi, :], _NT, preferred_element_type=F32)
              + _table_tile(tab_ref, r0, k_lo, k_hi, s_len))
        o, l = _weighted_values(_exp2_weights(s2) * _table_tile(mult_ref, r0, k_lo, k_hi, s_len),
                                vb_ref[k_lo:k_hi, :])
        o_ref[r0:r0 + ATT_ROWS, :] = (o * (1.0 / l)).astype(o_ref.dtype)


def dilated_attention(p, col0, g_qk, dil_tab, mult_tab):
    b, s, _ = p.shape
    vec = pl.BlockSpec((1, HEAD_DIM), lambda h, bi: (0, 0))
    return pl.pallas_call(
        _dilated_attn_kernel,
        grid=(N_HEADS, b),
        in_specs=_head_specs(s, col0) + [
            vec, vec,
            pl.BlockSpec((None,) + dil_tab.shape[1:], lambda h, bi: (h, 0, 0)),
            pl.BlockSpec(mult_tab.shape, lambda h, bi: (0, 0))],
        out_specs=pl.BlockSpec((None, s, HEAD_DIM), lambda h, bi: (bi, 0, h)),
        out_shape=jax.ShapeDtypeStruct((b, s, GROUP_WIDTH), BF16),
        scratch_shapes=[pltpu.VMEM((s, HEAD_DIM), BF16), pltpu.VMEM((s, 2 * HEAD_DIM), BF16)],
        compiler_params=_params(2),
        name="dilated_attention",
    )(p, p, p, g_qk[0].reshape(1, HEAD_DIM), g_qk[1].reshape(1, HEAD_DIM), dil_tab, mult_tab)


def _na_window_start(q_row0, n_rows):
    return min(max(q_row0 - NA_ROWS // 2, 0), n_rows - NA_K_ROWS)


def _na_attn_kernel(q_ref, k_ref, v_ref, gq_ref, gk_ref, bias_ref, o_ref, kn_ref, vb_ref):
    n_rows = k_ref.shape[0] // GRID_W
    kn_ref[...] = _rms_lanes(k_ref[...], gk_ref[...]).astype(BF16)
    vb_ref[...] = _values_with_ones(v_ref[...])
    tq, tk = NA_Q_ROWS * GRID_W, NA_K_ROWS * GRID_W
    for qb in range(n_rows // NA_Q_ROWS):
        k0 = _na_window_start(qb * NA_Q_ROWS, n_rows) * GRID_W
        qn = (_rms_lanes(q_ref[qb * tq:(qb + 1) * tq, :], gq_ref[...]) * (HEAD_DIM ** -0.5 * LOG2E)).astype(BF16)
        p = _exp2_weights(lax.dot_general(qn, kn_ref[k0:k0 + tk, :], _NT, preferred_element_type=F32)
                          + bias_ref[qb])
        o, l = _weighted_values(p, vb_ref[k0:k0 + tk, :])
        o_ref[qb * tq:(qb + 1) * tq, :] = (o * (1.0 / l)).astype(o_ref.dtype)


def _na_bias_kernel(t_ref, o_ref, *, n_rows):
    lane = lax.broadcasted_iota(I32, (GRID_W, 2 * GRID_W), 1)
    for qb in range(n_rows // NA_Q_ROWS):
        q_row0 = qb * NA_Q_ROWS
        k_row0 = _na_window_start(q_row0, n_rows)
        for a in range(NA_Q_ROWS):
            rq = q_row0 + a
            rs = min(max(rq - NA_ROWS // 2, 0), n_rows - NA_ROWS)
            for wp in range(NA_K_ROWS // 2):
                rk = k_row0 + 2 * wp
                ok0, ok1 = rs <= rk < rs + NA_ROWS, rs <= rk + 1 < rs + NA_ROWS
                m = min(max(rk - rq + NA_ROWS, 0), 2 * NA_ROWS - 1)
                tile = t_ref[m] * LOG2E
                if not (ok0 and ok1):
                    keep = (lane < GRID_W) if ok0 else (lane >= GRID_W)
                    tile = jnp.where(keep, tile, NEG_INF) if (ok0 or ok1) else jnp.full_like(tile, NEG_INF)
                o_ref[qb, a * GRID_W:(a + 1) * GRID_W, wp * 2 * GRID_W:(wp + 1) * 2 * GRID_W] = tile


def _na_bias(rpb, n_rows):
    n_rel_r = 2 * NA_ROWS - 1
    pad = GRID_W - NA_COLS
    rp = jnp.pad(rpb, ((0, 0), (0, 0), (pad, pad)))
    toe = jnp.stack([rp[:, :, NA_COLS - 1 + pad - cq:NA_COLS - 1 + pad - cq + GRID_W] for cq in range(GRID_W)],
                    axis=2)
    cq = np.arange(GRID_W)
    cs = np.clip(cq - NA_COLS // 2, 0, GRID_W - NA_COLS)
    col_ok = (cq[None, :] >= cs[:, None]) & (cq[None, :] < cs[:, None] + NA_COLS)
    toe = jnp.where(col_ok[None, None], toe, NEG_INF)
    ext = jnp.pad(toe, ((0, 0), (1, 1), (0, 0), (0, 0)), constant_values=NEG_INF)
    pairs = jnp.concatenate([ext[:, :n_rel_r + 1], ext[:, 1:]], axis=-1)
    tq, tk = NA_Q_ROWS * GRID_W, NA_K_ROWS * GRID_W
    nqb = n_rows // NA_Q_ROWS
    return pl.pallas_call(
        functools.partial(_na_bias_kernel, n_rows=n_rows),
        grid=(N_HEADS,),
        in_specs=[pl.BlockSpec((None, n_rel_r + 1, GRID_W, 2 * GRID_W), lambda h: (h, 0, 0, 0))],
        out_specs=pl.BlockSpec((None, nqb, tq, tk), lambda h: (h, 0, 0, 0)),
        out_shape=jax.ShapeDtypeStruct((N_HEADS, nqb, tq, tk), F32),
        compiler_params=_params(1),
        name="na_bias",
    )(pairs)


def neighbourhood_attention(p, col0, g_qk, rpb):
    b, s, _ = p.shape
    bias = _na_bias(rpb, s // GRID_W)
    vec = pl.BlockSpec((1, HEAD_DIM), lambda h, bi: (0, 0))
    return pl.pallas_call(
        _na_attn_kernel,
        grid=(N_HEADS, b),
        in_specs=_head_specs(s, col0) + [
            vec, vec,
            pl.BlockSpec((None,) + bias.shape[1:], lambda h, bi: (h, 0, 0, 0))],
        out_specs=pl.BlockSpec((None, s, HEAD_DIM), lambda h, bi: (bi, 0, h)),
        out_shape=jax.ShapeDtypeStruct((b, s, GROUP_WIDTH), BF16),
        scratch_shapes=[pltpu.VMEM((s, HEAD_DIM), BF16), pltpu.VMEM((s, 2 * HEAD_DIM), BF16)],
        compiler_params=_params(2),
        name="neighbourhood_attention",
    )(p, p, p, g_qk[0].reshape(1, HEAD_DIM), g_qk[1].reshape(1, HEAD_DIM), bias)


def _mla_norm_rope(x, g, cos, sin):
    ms = jnp.sum(x * x, axis=-1, keepdims=True) * (1.0 / MLA_QK)
    xn = x * lax.rsqrt(ms + EPS) * g
    xr = xn[:, HEAD_DIM:]
    partner = pltpu.roll(xr, QK_ROPE // 2, 1) + pltpu.roll(xr, HEAD_DIM - QK_ROPE // 2, 1)
    return jnp.concatenate([xn[:, :HEAD_DIM], xr * cos + partner * sin], axis=-1)


def _mla_attn_kernel(q_ref, kn_in_ref, v_ref, kr_ref, cos_ref, sin_ref, gq_ref, gk_ref, o_ref, kf_ref, vb_ref):
    s_len = v_ref.shape[0]
    k = jnp.concatenate([kn_in_ref[...], kr_ref[...]], axis=-1)
    kf_ref[...] = _mla_norm_rope(k, gk_ref[...], cos_ref[...], sin_ref[...]).astype(BF16)
    vb_ref[...] = _values_with_ones(v_ref[...])
    def scores(r0):
        rows = slice(r0, r0 + ATT_ROWS)
        qf = (_mla_norm_rope(q_ref[rows, :], gq_ref[...], cos_ref[rows, :], sin_ref[rows, :])
              * (MLA_QK ** -0.5 * LOG2E)).astype(BF16)
        return lax.dot_general(qf, kf_ref[...], _NT, preferred_element_type=F32)

    s2_next = scores(0)
    for r0 in range(0, s_len, ATT_ROWS):
        s2 = s2_next
        if r0 + ATT_ROWS < s_len:
            s2_next = scores(r0 + ATT_ROWS)
        o, l = _weighted_values(_exp2_weights(s2), vb_ref[...])
        o_ref[r0:r0 + ATT_ROWS, :] = (o * (1.0 / l)).astype(o_ref.dtype)


def mla_attention(q_up, kv_up, k_rope, g_qk, cos_t, sin_t):
    b, s, _ = q_up.shape
    pad = MLA_QK_PAD - MLA_QK
    gq = jnp.pad(g_qk[0], (0, pad)).reshape(1, MLA_QK_PAD)
    gk = jnp.pad(g_qk[1], (0, pad)).reshape(1, MLA_QK_PAD)
    vec = pl.BlockSpec((1, MLA_QK_PAD), lambda bi, h: (0, 0))
    rope = pl.BlockSpec((s, HEAD_DIM), lambda bi, h: (0, 0))
    return pl.pallas_call(
        _mla_attn_kernel,
        grid=(b, N_HEADS),
        in_specs=[pl.BlockSpec((None, s, MLA_QK_PAD), lambda bi, h: (bi, 0, h)),
                  pl.BlockSpec((None, s, HEAD_DIM), lambda bi, h: (bi, 0, 2 * h)),
                  pl.BlockSpec((None, s, HEAD_DIM), lambda bi, h: (bi, 0, 2 * h + 1)),
                  pl.BlockSpec((None, s, HEAD_DIM), lambda bi, h: (bi, 0, 0)),
                  rope, rope, vec, vec],
        out_specs=pl.BlockSpec((None, s, HEAD_DIM), lambda bi, h: (bi, 0, h)),
        out_shape=jax.ShapeDtypeStruct((b, s, GROUP_WIDTH), BF16),
        scratch_shapes=[pltpu.VMEM((s, MLA_QK_PAD), BF16), pltpu.VMEM((s, 2 * HEAD_DIM), BF16)],
        compiler_params=_params(2),
        name="mla_attention",
    )(q_up, kv_up, kv_up, k_rope, cos_t, sin_t, gq, gk)


def _top_k_rows(s, order, payload=None):
    n, tt = s.shape
    out_row = lax.broadcasted_iota(I32, (PEER_TOPK, tt), 0)
    vals = jnp.zeros((PEER_TOPK, tt), F32)
    picks = jnp.zeros((PEER_TOPK, tt), I32)
    for r in range(PEER_TOPK):
        m = jnp.max(s, axis=0, keepdims=True)
        first = jnp.min(jnp.where(s == m, order, jnp.iinfo(jnp.int32).max), axis=0, keepdims=True)
        sel = order == first
        vals = jnp.where(out_row == r, m, vals)
        pick = first if payload is None else jnp.max(jnp.where(sel, payload, 0), axis=0, keepdims=True)
        picks = jnp.where(out_row == r, pick, picks)
        s = jnp.where(sel, -jnp.inf, s)
    return vals, picks


_PAIR_BLOCKS = (("i", 16, 0), ("i", 8, 1), ("i", 8, 2), ("i", 8, 3), ("j", 16, 0), ("j", 8, 1), ("j", 8, 2))
_PAIR_SPLIT = 4


def _pair_candidates(s0, i0, s1, i1):
    tt = s0.shape[1]
    scores, experts, flats = [], [], []
    for vary, n, fixed in _PAIR_BLOCKS:
        r = lax.broadcasted_iota(I32, (n, tt), 0)
        limit = PEER_TOPK // (fixed + 1)
        if vary == "i":
            sc = s0[:n] + s1[fixed:fixed + 1]
            ex = i0[:n] * PEER_NKEYS + i1[fixed:fixed + 1]
            flat = r * PEER_TOPK + fixed
            valid = r < limit
        else:
            sc = s0[fixed:fixed + 1] + s1[:n]
            ex = i0[fixed:fixed + 1] * PEER_NKEYS + i1[:n]
            flat = fixed * PEER_TOPK + r
            valid = (r < limit) & (r >= _PAIR_SPLIT)
        scores.append(jnp.where(valid, sc, -jnp.inf))
        experts.append(ex)
        flats.append(flat)
    return jnp.concatenate(scores, 0), jnp.concatenate(experts, 0), jnp.concatenate(flats, 0)


def _peer_route_kernel(q_ref, keys_ref, expert_ref, gate_ref):
    tt = q_ref.shape[0]
    lanes = HEAD_DIM
    key_row = lax.broadcasted_iota(I32, (PEER_NKEYS, lanes), 0)
    for c0 in range(0, tt, lanes):
        tops = []
        for c in range(2):
            qc = q_ref[c0:c0 + lanes, c * HEAD_DIM:(c + 1) * HEAD_DIM].astype(BF16)
            sc = lax.dot_general(keys_ref[c], qc, _NT, preferred_element_type=F32)
            tops.append(_top_k_rows(sc, key_row))
        (s0, i0), (s1, i1) = tops
        cand_s, cand_e, cand_flat = _pair_candidates(s0, i0, s1, i1)
        best_s, best_e = _top_k_rows(cand_s, cand_flat, cand_e)
        e = jnp.exp(best_s - jnp.max(best_s, axis=0, keepdims=True))
        gate_ref[:, c0:c0 + lanes] = e * (1.0 / jnp.sum(e, axis=0, keepdims=True))
        expert_ref[:, c0:c0 + lanes] = best_e


def peer_route(q, sub_keys, tt=256):
    t = q.shape[0]
    n_sel = PEER_HEADS * PEER_TOPK
    return pl.pallas_call(
        _peer_route_kernel,
        grid=(t // tt, PEER_HEADS),
        in_specs=[pl.BlockSpec((tt, 2 * HEAD_DIM), lambda i, h: (i, h)),
                  pl.BlockSpec((2, PEER_NKEYS, HEAD_DIM), lambda i, h: (h, 0, 0))],
        out_specs=[pl.BlockSpec((PEER_TOPK, tt), lambda i, h: (h, i)),
                   pl.BlockSpec((PEER_TOPK, tt), lambda i, h: (h, i))],
        out_shape=[jax.ShapeDtypeStruct((n_sel, t), I32), jax.ShapeDtypeStruct((n_sel, t), F32)],
        compiler_params=_params(2),
        name="peer_route",
    )(q, sub_keys)


def _peer_gate_matrix_kernel(expert_ref, gate_ref, o_ref, e_s, g_s):
    e_s[...] = expert_ref[...].T
    g_s[...] = gate_ref[...].T
    n_sel = e_s.shape[1]
    row = lax.broadcasted_iota(I32, (PEER_NKEYS, n_sel), 0)

    def body(t, carry):
        e_row = e_s[pl.ds(t, 1), :]
        g_row = g_s[pl.ds(t, 1), :]
        hi = jnp.where(row == (e_row >> 7), g_row, 0.0).astype(BF16)
        lo = jnp.where(row == (e_row & (PEER_NKEYS - 1)), 1.0, 0.0).astype(BF16)
        o_ref[t] = lax.dot_general(hi, lo, _NT, preferred_element_type=F32)
        return carry

    lax.fori_loop(0, e_s.shape[0], body, 0, unroll=16)


def peer_gate_matrix(expert_t, gate_t, tg=128):
    n_sel, t = expert_t.shape
    return pl.pallas_call(
        _peer_gate_matrix_kernel,
        grid=(t // tg,),
        in_specs=[pl.BlockSpec((n_sel, tg), lambda i: (0, i)), pl.BlockSpec((n_sel, tg), lambda i: (0, i))],
        out_specs=pl.BlockSpec((tg, PEER_NKEYS, PEER_NKEYS), lambda i: (i, 0, 0)),
        out_shape=jax.ShapeDtypeStruct((t, PEER_NKEYS, PEER_NKEYS), F32),
        scratch_shapes=[pltpu.VMEM((tg, n_sel), I32), pltpu.VMEM((tg, n_sel), F32)],
        compiler_params=_params(1),
        name="peer_gate_matrix",
    )(expert_t, gate_t)


def _rope_tables(s):
    inv_freq = 1.0 / (ROPE_THETA ** (np.arange(0, QK_ROPE, 2, dtype=np.float32) / QK_ROPE))
    ang = jnp.arange(s, dtype=F32)[:, None] * jnp.asarray(inv_freq, F32)[None, :]
    cos, sin = jnp.cos(ang), jnp.sin(ang)
    zeros = jnp.zeros((s, HEAD_DIM - QK_ROPE), F32)
    return (jnp.concatenate([cos, cos, zeros], axis=-1), jnp.concatenate([-sin, sin, zeros], axis=-1))


def _layer(x2, b, s, l, tables, g_mix, w_in, w_out, diff_lambda, diff_g_qk, diff_g_out, dil_g_qk, na_g_qk,
           na_rpb, mla_g_cq, mla_g_ckv, mla_w_uq, mla_w_ukv, mla_g_qk, g_ffn, peer_w_q, peer_sub_keys, peer_u,
           peer_v):
    t = b * s
    cos_t, sin_t, alibi, dil_tab, mult_tab = tables
    lambda_init = 0.8 - 0.6 * math.exp(-0.3 * l)
    n_qkv = 3 * QKV_COLS

    h = rms_rows(x2, g_mix)
    p = matmul(h, w_in[:, :n_qkv].astype(BF16)).reshape(b, s, n_qkv)
    w_lat = jnp.pad(w_in[:, n_qkv:], ((0, 0), (0, HEAD_DIM - QK_ROPE))).astype(BF16)
    p_lat = matmul(h, w_lat, tn=MLA_LATENT_PAD)
    c_q, c_kv, k_rope = mla_latent_norm(p_lat, mla_g_cq, mla_g_ckv)

    blocks = QKV_COLS // HEAD_DIM
    o_a = diff_attention(p, 0, diff_lambda, diff_g_qk, diff_g_out, lambda_init, alibi)
    o_b = dilated_attention(p, blocks, dil_g_qk, dil_tab, mult_tab)
    o_c = neighbourhood_attention(p, 2 * blocks, na_g_qk, na_rpb)

    w_uq = jnp.pad(mla_w_uq.reshape(Q_LORA, N_HEADS, MLA_QK), ((0, 0), (0, 0), (0, MLA_QK_PAD - MLA_QK)))
    q_up = matmul(c_q, w_uq.reshape(Q_LORA, N_HEADS * MLA_QK_PAD).astype(BF16), tm=1024)
    kv_up = matmul(c_kv, mla_w_ukv.astype(BF16), tm=1024)
    o_d = mla_attention(q_up.reshape(b, s, -1), kv_up.reshape(b, s, -1), k_rope.reshape(b, s, HEAD_DIM),
                        mla_g_qk, cos_t, sin_t)

    mixed = jnp.concatenate([o_a, o_b, o_c, o_d], axis=-1).reshape(t, 4 * GROUP_WIDTH)
    x2 = matmul(mixed, w_out.astype(BF16), residual=x2)

    hf = rms_rows(x2, g_ffn)
    q = matmul(hf, peer_w_q.astype(BF16))
    keys = peer_sub_keys.reshape(2 * PEER_HEADS, PEER_NKEYS, HEAD_DIM).astype(BF16)
    expert_t, gate_t = peer_route(q, keys)
    gates = peer_gate_matrix(expert_t, gate_t)
    act = peer_gated_act(hf, peer_u.astype(BF16), gates)
    return matmul_res_ktiled(act, peer_v.astype(BF16), x2)


def kernel(x, g_mix, w_in, w_out, diff_lambda, diff_g_qk, diff_g_out, dil_g_qk, na_g_qk, na_rpb, mla_g_cq,
           mla_g_ckv, mla_w_uq, mla_w_ukv, mla_g_qk, g_ffn, peer_w_q, peer_sub_keys, peer_u, peer_v):
    b, s, d = x.shape
    depth = g_mix.shape[0]
    tables = _rope_tables(s) + tuple(alibi_tables(s))
    x2 = x.reshape(b * s, d)
    per_layer = (g_mix, w_in, w_out, diff_lambda, diff_g_qk, diff_g_out, dil_g_qk, na_g_qk, na_rpb, mla_g_cq,
                 mla_g_ckv, mla_w_uq, mla_w_ukv, mla_g_qk, g_ffn, peer_w_q, peer_sub_keys, peer_u, peer_v)
    for l in range(depth):
        x2 = _layer(x2, b, s, l, tables, *(w[l] for w in per_layer))
    return x2.reshape(b, s, d)
```

```python
import functools
import math

import jax
import jax.numpy as jnp
import numpy as np
from jax import lax
from jax.experimental import pallas as pl
from jax.experimental.pallas import tpu as pltpu

F32 = jnp.float32
BF16 = jnp.bfloat16
I32 = jnp.int32

D_MODEL = 4096
HEAD_DIM = 128
N_HEADS = 8
GROUP_WIDTH = N_HEADS * HEAD_DIM
QKV_COLS = 3 * GROUP_WIDTH
EPS = 1e-6
NEG_INF = -1e30
LOG2E = math.log2(math.e)
DIFF_DH = HEAD_DIM // 2
DIL_HALF = 64
DIL_DILATIONS = (1, 4, 16)
DIL_REACH = DIL_HALF * max(DIL_DILATIONS)
GRID_W = 64
NA_ROWS = 8
NA_COLS = 16
NA_Q_ROWS = 4
NA_K_ROWS = NA_Q_ROWS + NA_ROWS
Q_LORA = 768
KV_LORA = 512
QK_NOPE = 128
QK_ROPE = 64
MLA_QK = QK_NOPE + QK_ROPE
MLA_QK_PAD = 2 * HEAD_DIM
ROPE_THETA = 10000.0
PEER_HEADS = 8
PEER_NKEYS = 128
PEER_TOPK = 16
PEER_EXPERTS = PEER_NKEYS * PEER_NKEYS
MLA_LATENT_PAD = Q_LORA + KV_LORA + HEAD_DIM
ATT_ROWS = 256
TAB_ROWS = 512

VMEM_LIMIT = 48 * 1024 * 1024

_NT = (((1,), (1,)), ((), ()))


def _params(grid_rank):
    return pltpu.CompilerParams(dimension_semantics=("arbitrary",) * grid_rank, vmem_limit_bytes=VMEM_LIMIT)


def _rms_rows_kernel(x_ref, g_ref, o_ref):
    x = x_ref[...]
    ms = jnp.mean(x * x, axis=-1, keepdims=True)
    o_ref[...] = (x * lax.rsqrt(ms + EPS) * g_ref[...]).astype(o_ref.dtype)


def rms_rows(x, g, tm=256):
    t, c = x.shape
    return pl.pallas_call(
        _rms_rows_kernel,
        grid=(t // tm,),
        in_specs=[pl.BlockSpec((tm, c), lambda i: (i, 0)), pl.BlockSpec((1, c), lambda i: (0, 0))],
        out_specs=pl.BlockSpec((tm, c), lambda i: (i, 0)),
        out_shape=jax.ShapeDtypeStruct((t, c), BF16),
        compiler_params=_params(1),
        name="rms_rows",
    )(x, g.reshape(1, c))


def _mla_latent_kernel(p_ref, gq_ref, gkv_ref, cq_ref, ckv_ref, kr_ref):
    cq = p_ref[:, :Q_LORA]
    ckv = p_ref[:, Q_LORA:Q_LORA + KV_LORA]
    cq_ref[...] = (cq * lax.rsqrt(jnp.mean(cq * cq, axis=-1, keepdims=True) + EPS) * gq_ref[...]).astype(BF16)
    ckv_ref[...] = (ckv * lax.rsqrt(jnp.mean(ckv * ckv, axis=-1, keepdims=True) + EPS) * gkv_ref[...]).astype(BF16)
    kr_ref[...] = p_ref[:, Q_LORA + KV_LORA:]


def mla_latent_norm(p_lat, g_cq, g_ckv, tm=512):
    t = p_lat.shape[0]
    return pl.pallas_call(
        _mla_latent_kernel,
        grid=(t // tm,),
        in_specs=[pl.BlockSpec((tm, MLA_LATENT_PAD), lambda i: (i, 0)),
                  pl.BlockSpec((1, Q_LORA), lambda i: (0, 0)),
                  pl.BlockSpec((1, KV_LORA), lambda i: (0, 0))],
        out_specs=[pl.BlockSpec((tm, Q_LORA), lambda i: (i, 0)),
                   pl.BlockSpec((tm, KV_LORA), lambda i: (i, 0)),
                   pl.BlockSpec((tm, HEAD_DIM), lambda i: (i, 0))],
        out_shape=[jax.ShapeDtypeStruct((t, Q_LORA), BF16),
                   jax.ShapeDtypeStruct((t, KV_LORA), BF16),
                   jax.ShapeDtypeStruct((t, HEAD_DIM), F32)],
        compiler_params=_params(1),
        name="mla_latent_norm",
    )(p_lat, g_cq.reshape(1, Q_LORA), g_ckv.reshape(1, KV_LORA))


def _mm_kernel(a_ref, b_ref, o_ref):
    o_ref[...] = jnp.dot(a_ref[...], b_ref[...], preferred_element_type=F32).astype(o_ref.dtype)


def _mm_res_kernel(a_ref, b_ref, r_ref, o_ref):
    o_ref[...] = r_ref[...] + jnp.dot(a_ref[...], b_ref[...], preferred_element_type=F32)


def _mm_res_acc_kernel(a_ref, b_ref, r_ref, o_ref):
    @pl.when(pl.program_id(2) == 0)
    def _():
        o_ref[...] = r_ref[...]

    o_ref[...] += jnp.dot(a_ref[...], b_ref[...], preferred_element_type=F32)


def matmul(a, b, *, out_dtype=F32, residual=None, tm=512, tn=1024):
    m, kd = a.shape
    n = b.shape[1]
    tm, tn = min(tm, m), min(tn, n)
    in_specs = [pl.BlockSpec((tm, kd), lambda j, i: (i, 0)),
                pl.BlockSpec((kd, tn), lambda j, i: (0, j))]
    args = [a, b]
    body = _mm_kernel
    if residual is not None:
        body = _mm_res_kernel
        in_specs.append(pl.BlockSpec((tm, tn), lambda j, i: (i, j)))
        args.append(residual)
    return pl.pallas_call(
        body,
        grid=(n // tn, m // tm),
        in_specs=in_specs,
        out_specs=pl.BlockSpec((tm, tn), lambda j, i: (i, j)),
        out_shape=jax.ShapeDtypeStruct((m, n), out_dtype),
        compiler_params=_params(2),
        name="matmul_res" if residual is not None else "matmul",
    )(*args)


def matmul_res_ktiled(a, b, residual, tm=1024, tn=1024, tk=2048):
    m, kd = a.shape
    n = b.shape[1]
    return pl.pallas_call(
        _mm_res_acc_kernel,
        grid=(m // tm, n // tn, kd // tk),
        in_specs=[pl.BlockSpec((tm, tk), lambda i, j, k: (i, k)),
                  pl.BlockSpec((tk, tn), lambda i, j, k: (k, j)),
                  pl.BlockSpec((tm, tn), lambda i, j, k: (i, j))],
        out_specs=pl.BlockSpec((tm, tn), lambda i, j, k: (i, j)),
        out_shape=jax.ShapeDtypeStruct((m, n), F32),
        compiler_params=_params(3),
        name="matmul_res_ktiled",
    )(a, b, residual)


def _gelu_exact(x):
    return 0.5 * x * (1.0 + lax.erf(x * (1.0 / math.sqrt(2.0))))


def _peer_act_kernel(h_ref, u_ref, g_ref, o_ref):
    tm, n_key0, n_key1 = g_ref.shape
    sub = 8
    g = jnp.swapaxes(g_ref[...].reshape(tm // sub, sub, n_key0, n_key1), 1, 2)
    act = _gelu_exact(lax.dot_general(h_ref[...], u_ref[...], _NT, preferred_element_type=F32))
    for a in range(n_key0):
        cols = slice(a * n_key1, (a + 1) * n_key1)
        o_ref[:, cols] = (act[:, cols] * g[:, a].reshape(tm, n_key1)).astype(o_ref.dtype)


def peer_gated_act(hn, u, gates, tm=512, tn=1024):
    t, d = hn.shape
    e = u.shape[0]
    tm = min(tm, t)
    n_key0 = tn // PEER_NKEYS
    return pl.pallas_call(
        _peer_act_kernel,
        grid=(e // tn, t // tm),
        in_specs=[pl.BlockSpec((tm, d), lambda j, i: (i, 0)),
                  pl.BlockSpec((tn, d), lambda j, i: (j, 0)),
                  pl.BlockSpec((tm, n_key0, PEER_NKEYS), lambda j, i: (i, j, 0))],
        out_specs=pl.BlockSpec((tm, tn), lambda j, i: (i, j)),
        out_shape=jax.ShapeDtypeStruct((t, e), BF16),
        compiler_params=_params(2),
        name="peer_gated_act",
    )(hn, u, gates)


def _rms_lanes(x, g):
    ms = jnp.mean(x * x, axis=-1, keepdims=True)
    return x * lax.rsqrt(ms + EPS) * g


def _rms_halves(x, g):
    lane = lax.broadcasted_iota(I32, x.shape, 1)
    lo = lane < DIFF_DH
    xx = x * x
    ms_lo = jnp.sum(jnp.where(lo, xx, 0.0), axis=-1, keepdims=True) * (1.0 / DIFF_DH)
    ms_hi = jnp.sum(jnp.where(lo, 0.0, xx), axis=-1, keepdims=True) * (1.0 / DIFF_DH)
    inv = jnp.where(lo, lax.rsqrt(ms_lo + EPS), lax.rsqrt(ms_hi + EPS))
    return x * inv * g


def _exp2_weights(s2):
    return jnp.exp2(s2 - jnp.max(s2, axis=-1, keepdims=True))


def _values_with_ones(v):
    ones_col = jnp.where(lax.broadcasted_iota(I32, v.shape, 1) == 0, 1.0, 0.0)
    return jnp.concatenate([v, ones_col], axis=-1).astype(BF16)


def _weighted_values(p, vx):
    ov = jnp.dot(p.astype(BF16), vx, preferred_element_type=F32)
    return ov[:, :HEAD_DIM], ov[:, HEAD_DIM:HEAD_DIM + 1]


def _alibi_tables_kernel(slope_ref, alibi_ref, dil_ref, mult_ref, *, s_len):
    shape = alibi_ref.shape
    d = lax.broadcasted_iota(I32, shape, 1) - lax.broadcasted_iota(I32, shape, 0) - (s_len - TAB_ROWS)
    ad = jnp.abs(d)
    alibi = (-LOG2E * slope_ref[pl.program_id(0)]) * ad.astype(F32)
    mult = jnp.zeros(shape, F32)
    for dil in DIL_DILATIONS:
        mult = mult + jnp.where(((d & (dil - 1)) == 0) & (ad <= DIL_HALF * dil), 1.0, 0.0)
    alibi_ref[...] = alibi
    dil_ref[...] = jnp.where(mult > 0.0, alibi, NEG_INF)
    mult_ref[...] = mult


def alibi_tables(s_len):
    slopes = jnp.asarray(2.0 ** (-8.0 * np.arange(1, N_HEADS + 1) / N_HEADS), F32)
    width = 2 * s_len - TAB_ROWS
    per_head = pl.BlockSpec((None, TAB_ROWS, width), lambda h: (h, 0, 0))
    return pl.pallas_call(
        functools.partial(_alibi_tables_kernel, s_len=s_len),
        grid=(N_HEADS,),
        in_specs=[pl.BlockSpec(memory_space=pltpu.SMEM)],
        out_specs=[per_head, per_head, pl.BlockSpec((TAB_ROWS, width), lambda h: (0, 0))],
        out_shape=[jax.ShapeDtypeStruct((N_HEADS, TAB_ROWS, width), F32),
                   jax.ShapeDtypeStruct((N_HEADS, TAB_ROWS, width), F32),
                   jax.ShapeDtypeStruct((TAB_ROWS, width), F32)],
        compiler_params=_params(1),
        name="alibi_tables",
    )(slopes)


def _table_tile(tab_ref, r0, k_lo, k_hi, s_len):
    row = r0 % TAB_ROWS
    lane0 = s_len - TAB_ROWS - (r0 - row) + k_lo
    return tab_ref[row:row + ATT_ROWS, lane0:lane0 + (k_hi - k_lo)]


def _head_specs(s, col0, width=HEAD_DIM):
    return [pl.BlockSpec((None, s, width), lambda h, bi, c=c: (bi, 0, col0 + c * N_HEADS + h)) for c in range(3)]


def _diff_attn_kernel(q_ref, k_ref, v_ref, gq_ref, gk_ref, go_ref, lam_ref, tab_ref, o_ref, kn_ref, vb_ref,
                      *, lambda_init):
    s_len = k_ref.shape[0]
    kn_ref[...] = _rms_halves(k_ref[...], gk_ref[...]).astype(BF16)
    vb_ref[...] = _values_with_ones(v_ref[...])
    lv = lam_ref[...]
    lam = (jnp.exp(jnp.sum(lv[0:1] * lv[1:2], axis=-1, keepdims=True))
           - jnp.exp(jnp.sum(lv[2:3] * lv[3:4], axis=-1, keepdims=True)) + lambda_init)
    lo = lax.broadcasted_iota(I32, (ATT_ROWS, HEAD_DIM), 1) < DIFF_DH
    for r0 in range(0, s_len, ATT_ROWS):
        qn = _rms_halves(q_ref[r0:r0 + ATT_ROWS, :], gq_ref[...]) * (DIFF_DH ** -0.5 * LOG2E)
        tab = _table_tile(tab_ref, r0, 0, s_len, s_len)
        p0 = _exp2_weights(lax.dot_general(jnp.where(lo, qn, 0.0).astype(BF16), kn_ref[...], _NT,
                                           preferred_element_type=F32) + tab)
        p1 = _exp2_weights(lax.dot_general(jnp.where(lo, 0.0, qn).astype(BF16), kn_ref[...], _NT,
                                           preferred_element_type=F32) + tab)
        o0, l0 = _weighted_values(p0, vb_ref[...])
        o1, l1 = _weighted_values(p1, vb_ref[...])
        o = o0 * (1.0 / l0) - o1 * (lam / l1)
        o_ref[r0:r0 + ATT_ROWS, :] = (_rms_lanes(o, go_ref[...]) * (1.0 - lambda_init)).astype(o_ref.dtype)


def diff_attention(p, col0, lam_vecs, g_qk, g_out, lambda_init, alibi):
    b, s, _ = p.shape
    lam_pad = jnp.pad(lam_vecs, ((0, 0), (0, HEAD_DIM - DIFF_DH)))
    gq = jnp.tile(g_qk[0], 2).reshape(1, HEAD_DIM)
    gk = jnp.tile(g_qk[1], 2).reshape(1, HEAD_DIM)
    vec = pl.BlockSpec((1, HEAD_DIM), lambda h, bi: (0, 0))
    return pl.pallas_call(
        functools.partial(_diff_attn_kernel, lambda_init=lambda_init),
        grid=(N_HEADS, b),
        in_specs=_head_specs(s, col0) + [
            vec, vec, vec,
            pl.BlockSpec((4, HEAD_DIM), lambda h, bi: (0, 0)),
            pl.BlockSpec((None,) + alibi.shape[1:], lambda h, bi: (h, 0, 0))],
        out_specs=pl.BlockSpec((None, s, HEAD_DIM), lambda h, bi: (bi, 0, h)),
        out_shape=jax.ShapeDtypeStruct((b, s, GROUP_WIDTH), BF16),
        scratch_shapes=[pltpu.VMEM((s, HEAD_DIM), BF16), pltpu.VMEM((s, 2 * HEAD_DIM), BF16)],
        compiler_params=_params(2),
        name="diff_attention",
    )(p, p, p, gq, gk, g_out.reshape(1, HEAD_DIM), lam_pad, alibi)


def _dilated_attn_kernel(q_ref, k_ref, v_ref, gq_ref, gk_ref, tab_ref, mult_ref, o_ref, kn_ref, vb_ref):
    s_len = k_ref.shape[0]
    kn_ref[...] = _rms_lanes(k_ref[...], gk_ref[...]).astype(BF16)
    vb_ref[...] = _values_with_ones(v_ref[...])
    for r0 in range(0, s_len, ATT_ROWS):
        k_lo, k_hi = max(0, r0 - DIL_REACH), min(s_len, r0 + ATT_ROWS + DIL_REACH)
        qn = (_rms_lanes(q_ref[r0:r0 + ATT_ROWS, :], gq_ref[...]) * (HEAD_DIM ** -0.5 * LOG2E)).astype(BF16)
        s2 = (lax.dot_general(qn, kn_ref[k_lo:k_hi, :], _NT, preferred_element_type=F32)
              + _table_tile(tab_ref, r0, k_lo, k_hi, s_len))
        o, l = _weighted_values(_exp2_weights(s2) * _table_tile(mult_ref, r0, k_lo, k_hi, s_len),
                                vb_ref[k_lo:k_hi, :])
        o_ref[r0:r0 + ATT_ROWS, :] = (o * (1.0 / l)).astype(o_ref.dtype)


def dilated_attention(p, col0, g_qk, dil_tab, mult_tab):
    b, s, _ = p.shape
    vec = pl.BlockSpec((1, HEAD_DIM), lambda h, bi: (0, 0))
    return pl.pallas_call(
        _dilated_attn_kernel,
        grid=(N_HEADS, b),
        in_specs=_head_specs(s, col0) + [
            vec, vec,
            pl.BlockSpec((None,) + dil_tab.shape[1:], lambda h, bi: (h, 0, 0)),
            pl.BlockSpec(mult_tab.shape, lambda h, bi: (0, 0))],
        out_specs=pl.BlockSpec((None, s, HEAD_DIM), lambda h, bi: (bi, 0, h)),
        out_shape=jax.ShapeDtypeStruct((b, s, GROUP_WIDTH), BF16),
        scratch_shapes=[pltpu.VMEM((s, HEAD_DIM), BF16), pltpu.VMEM((s, 2 * HEAD_DIM), BF16)],
        compiler_params=_params(2),
        name="dilated_attention",
    )(p, p, p, g_qk[0].reshape(1, HEAD_DIM), g_qk[1].reshape(1, HEAD_DIM), dil_tab, mult_tab)


def _na_window_start(q_row0, n_rows):
    return min(max(q_row0 - NA_ROWS // 2, 0), n_rows - NA_K_ROWS)


def _na_attn_kernel(q_ref, k_ref, v_ref, gq_ref, gk_ref, bias_ref, o_ref, kn_ref, vb_ref):
    n_rows = k_ref.shape[0] // GRID_W
    kn_ref[...] = _rms_lanes(k_ref[...], gk_ref[...]).astype(BF16)
    vb_ref[...] = _values_with_ones(v_ref[...])
    tq, tk = NA_Q_ROWS * GRID_W, NA_K_ROWS * GRID_W
    for qb in range(n_rows // NA_Q_ROWS):
        k0 = _na_window_start(qb * NA_Q_ROWS, n_rows) * GRID_W
        qn = (_rms_lanes(q_ref[qb * tq:(qb + 1) * tq, :], gq_ref[...]) * (HEAD_DIM ** -0.5 * LOG2E)).astype(BF16)
        p = _exp2_weights(lax.dot_general(qn, kn_ref[k0:k0 + tk, :], _NT, preferred_element_type=F32)
                          + bias_ref[qb])
        o, l = _weighted_values(p, vb_ref[k0:k0 + tk, :])
        o_ref[qb * tq:(qb + 1) * tq, :] = (o * (1.0 / l)).astype(o_ref.dtype)


def _na_bias_kernel(t_ref, o_ref, *, n_rows):
    lane = lax.broadcasted_iota(I32, (GRID_W, 2 * GRID_W), 1)
    for qb in range(n_rows // NA_Q_ROWS):
        q_row0 = qb * NA_Q_ROWS
        k_row0 = _na_window_start(q_row0, n_rows)
        for a in range(NA_Q_ROWS):
            rq = q_row0 + a
            rs = min(max(rq - NA_ROWS // 2, 0), n_rows - NA_ROWS)
            for wp in range(NA_K_ROWS // 2):
                rk = k_row0 + 2 * wp
                ok0, ok1 = rs <= rk < rs + NA_ROWS, rs <= rk + 1 < rs + NA_ROWS
                m = min(max(rk - rq + NA_ROWS, 0), 2 * NA_ROWS - 1)
                tile = t_ref[m] * LOG2E
                if not (ok0 and ok1):
                    keep = (lane < GRID_W) if ok0 else (lane >= GRID_W)
                    tile = jnp.where(keep, tile, NEG_INF) if (ok0 or ok1) else jnp.full_like(tile, NEG_INF)
                o_ref[qb, a * GRID_W:(a + 1) * GRID_W, wp * 2 * GRID_W:(wp + 1) * 2 * GRID_W] = tile


def _na_bias(rpb, n_rows):
    n_rel_r = 2 * NA_ROWS - 1
    pad = GRID_W - NA_COLS
    rp = jnp.pad(rpb, ((0, 0), (0, 0), (pad, pad)))
    toe = jnp.stack([rp[:, :, NA_COLS - 1 + pad - cq:NA_COLS - 1 + pad - cq + GRID_W] for cq in range(GRID_W)],
                    axis=2)
    cq = np.arange(GRID_W)
    cs = np.clip(cq - NA_COLS // 2, 0, GRID_W - NA_COLS)
    col_ok = (cq[None, :] >= cs[:, None]) & (cq[None, :] < cs[:, None] + NA_COLS)
    toe = jnp.where(col_ok[None, None], toe, NEG_INF)
    ext = jnp.pad(toe, ((0, 0), (1, 1), (0, 0), (0, 0)), constant_values=NEG_INF)
    pairs = jnp.concatenate([ext[:, :n_rel_r + 1], ext[:, 1:]], axis=-1)
    tq, tk = NA_Q_ROWS * GRID_W, NA_K_ROWS * GRID_W
    nqb = n_rows // NA_Q_ROWS
    return pl.pallas_call(
        functools.partial(_na_bias_kernel, n_rows=n_rows),
        grid=(N_HEADS,),
        in_specs=[pl.BlockSpec((None, n_rel_r + 1, GRID_W, 2 * GRID_W), lambda h: (h, 0, 0, 0))],
        out_specs=pl.BlockSpec((None, nqb, tq, tk), lambda h: (h, 0, 0, 0)),
        out_shape=jax.ShapeDtypeStruct((N_HEADS, nqb, tq, tk), F32),
        compiler_params=_params(1),
        name="na_bias",
    )(pairs)


def neighbourhood_attention(p, col0, g_qk, rpb):
    b, s, _ = p.shape
    bias = _na_bias(rpb, s // GRID_W)
    vec = pl.BlockSpec((1, HEAD_DIM), lambda h, bi: (0, 0))
    return pl.pallas_call(
        _na_attn_kernel,
        grid=(N_HEADS, b),
        in_specs=_head_specs(s, col0) + [
            vec, vec,
            pl.BlockSpec((None,) + bias.shape[1:], lambda h, bi: (h, 0, 0, 0))],
        out_specs=pl.BlockSpec((None, s, HEAD_DIM), lambda h, bi: (bi, 0, h)),
        out_shape=jax.ShapeDtypeStruct((b, s, GROUP_WIDTH), BF16),
        scratch_shapes=[pltpu.VMEM((s, HEAD_DIM), BF16), pltpu.VMEM((s, 2 * HEAD_DIM), BF16)],
        compiler_params=_params(2),
        name="neighbourhood_attention",
    )(p, p, p, g_qk[0].reshape(1, HEAD_DIM), g_qk[1].reshape(1, HEAD_DIM), bias)


def _mla_norm_rope(x, g, cos, sin):
    ms = jnp.sum(x * x, axis=-1, keepdims=True) * (1.0 / MLA_QK)
    xn = x * lax.rsqrt(ms + EPS) * g
    xr = xn[:, HEAD_DIM:]
    partner = pltpu.roll(xr, QK_ROPE // 2, 1) + pltpu.roll(xr, HEAD_DIM - QK_ROPE // 2, 1)
    return jnp.concatenate([xn[:, :HEAD_DIM], xr * cos + partner * sin], axis=-1)


def _mla_attn_kernel(q_ref, kn_in_ref, v_ref, kr_ref, cos_ref, sin_ref, gq_ref, gk_ref, o_ref, kf_ref, vb_ref):
    s_len = v_ref.shape[0]
    k = jnp.concatenate([kn_in_ref[...], kr_ref[...]], axis=-1)
    kf_ref[...] = _mla_norm_rope(k, gk_ref[...], cos_ref[...], sin_ref[...]).astype(BF16)
    vb_ref[...] = _values_with_ones(v_ref[...])
    def scores(r0):
        rows = slice(r0, r0 + ATT_ROWS)
        qf = (_mla_norm_rope(q_ref[rows, :], gq_ref[...], cos_ref[rows, :], sin_ref[rows, :])
              * (MLA_QK ** -0.5 * LOG2E)).astype(BF16)
        return lax.dot_general(qf, kf_ref[...], _NT, preferred_element_type=F32)

    s2_next = scores(0)
    for r0 in range(0, s_len, ATT_ROWS):
        s2 = s2_next
        if r0 + ATT_ROWS < s_len:
            s2_next = scores(r0 + ATT_ROWS)
        o, l = _weighted_values(_exp2_weights(s2), vb_ref[...])
        o_ref[r0:r0 + ATT_ROWS, :] = (o * (1.0 / l)).astype(o_ref.dtype)


def mla_attention(q_up, kv_up, k_rope, g_qk, cos_t, sin_t):
    b, s, _ = q_up.shape
    pad = MLA_QK_PAD - MLA_QK
    gq = jnp.pad(g_qk[0], (0, pad)).reshape(1, MLA_QK_PAD)
    gk = jnp.pad(g_qk[1], (0, pad)).reshape(1, MLA_QK_PAD)
    vec = pl.BlockSpec((1, MLA_QK_PAD), lambda bi, h: (0, 0))
    rope = pl.BlockSpec((s, HEAD_DIM), lambda bi, h: (0, 0))
    return pl.pallas_call(
        _mla_attn_kernel,
        grid=(b, N_HEADS),
        in_specs=[pl.BlockSpec((None, s, MLA_QK_PAD), lambda bi, h: (bi, 0, h)),
                  pl.BlockSpec((None, s, HEAD_DIM), lambda bi, h: (bi, 0, 2 * h)),
                  pl.BlockSpec((None, s, HEAD_DIM), lambda bi, h: (bi, 0, 2 * h + 1)),
                  pl.BlockSpec((None, s, HEAD_DIM), lambda bi, h: (bi, 0, 0)),
                  rope, rope, vec, vec],
        out_specs=pl.BlockSpec((None, s, HEAD_DIM), lambda bi, h: (bi, 0, h)),
        out_shape=jax.ShapeDtypeStruct((b, s, GROUP_WIDTH), BF16),
        scratch_shapes=[pltpu.VMEM((s, MLA_QK_PAD), BF16), pltpu.VMEM((s, 2 * HEAD_DIM), BF16)],
        compiler_params=_params(2),
        name="mla_attention",
    )(q_up, kv_up, kv_up, k_rope, cos_t, sin_t, gq, gk)


def _top_k_lead(s, order, payload=None):
    vals, picks = [], []
    for _ in range(PEER_TOPK):
        m = jnp.max(s, axis=0, keepdims=True)
        first = jnp.min(jnp.where(s == m, order, jnp.iinfo(jnp.int32).max), axis=0, keepdims=True)
        sel = order == first
        vals.append(m)
        picks.append(first if payload is None else jnp.max(jnp.where(sel, payload, 0), axis=0, keepdims=True))
        s = jnp.where(sel, -jnp.inf, s)
    return jnp.concatenate(vals, axis=0), jnp.concatenate(picks, axis=0)


_PAIRS = tuple((i, j) for i in range(PEER_TOPK) for j in range(PEER_TOPK) if (i + 1) * (j + 1) <= PEER_TOPK)


def _peer_route_kernel(q_ref, keys_ref, expert_ref, gate_ref):
    groups = q_ref.shape[0] // HEAD_DIM
    tile = (1, groups, HEAD_DIM)
    key_id = lax.broadcasted_iota(I32, (PEER_NKEYS,) + tile[1:], 0)
    tops = []
    for c in range(2):
        cols = slice(c * HEAD_DIM, (c + 1) * HEAD_DIM)
        sc = jnp.stack([lax.dot_general(keys_ref[c], q_ref[g * HEAD_DIM:(g + 1) * HEAD_DIM, cols].astype(BF16), _NT,
                                        preferred_element_type=F32) for g in range(groups)], axis=0)
        tops.append(_top_k_lead(jnp.swapaxes(sc, 0, 1), key_id))
    (s0, i0), (s1, i1) = tops
    cand_s = jnp.concatenate([s0[i:i + 1] + s1[j:j + 1] for i, j in _PAIRS], axis=0)
    cand_e = jnp.concatenate([i0[i:i + 1] * PEER_NKEYS + i1[j:j + 1] for i, j in _PAIRS], axis=0)
    cand_flat = jnp.concatenate([jnp.full(tile, i * PEER_TOPK + j, I32) for i, j in _PAIRS], axis=0)
    best_s, best_e = _top_k_lead(cand_s, cand_flat, cand_e)
    e = jnp.exp(best_s - jnp.max(best_s, axis=0, keepdims=True))
    gate = jnp.swapaxes(e * (1.0 / jnp.sum(e, axis=0, keepdims=True)), 0, 1)
    expert = jnp.swapaxes(best_e, 0, 1)
    for g in range(groups):
        gate_ref[:, g * HEAD_DIM:(g + 1) * HEAD_DIM] = gate[g]
        expert_ref[:, g * HEAD_DIM:(g + 1) * HEAD_DIM] = expert[g]


def peer_route(q, sub_keys, tt=1024):
    t = q.shape[0]
    n_sel = PEER_HEADS * PEER_TOPK
    return pl.pallas_call(
        _peer_route_kernel,
        grid=(t // tt, PEER_HEADS),
        in_specs=[pl.BlockSpec((tt, 2 * HEAD_DIM), lambda i, h: (i, h)),
                  pl.BlockSpec((2, PEER_NKEYS, HEAD_DIM), lambda i, h: (h, 0, 0))],
        out_specs=[pl.BlockSpec((PEER_TOPK, tt), lambda i, h: (h, i)),
                   pl.BlockSpec((PEER_TOPK, tt), lambda i, h: (h, i))],
        out_shape=[jax.ShapeDtypeStruct((n_sel, t), I32), jax.ShapeDtypeStruct((n_sel, t), F32)],
        compiler_params=_params(2),
        name="peer_route",
    )(q, sub_keys)


def _peer_gate_matrix_kernel(expert_ref, gate_ref, o_ref, e_s, g_s):
    e_s[...] = expert_ref[...].T
    g_s[...] = gate_ref[...].T
    n_sel = e_s.shape[1]
    row = lax.broadcasted_iota(I32, (PEER_NKEYS, n_sel), 0)

    def body(t, carry):
        e_row = e_s[pl.ds(t, 1), :]
        g_row = g_s[pl.ds(t, 1), :]
        hi = jnp.where(row == (e_row >> 7), g_row, 0.0).astype(BF16)
        lo = jnp.where(row == (e_row & (PEER_NKEYS - 1)), 1.0, 0.0).astype(BF16)
        o_ref[t] = lax.dot_general(hi, lo, _NT, preferred_element_type=F32)
        return carry

    lax.fori_loop(0, e_s.shape[0], body, 0, unroll=16)


def peer_gate_matrix(expert_t, gate_t, tg=128):
    n_sel, t = expert_t.shape
    return pl.pallas_call(
        _peer_gate_matrix_kernel,
        grid=(t // tg,),
        in_specs=[pl.BlockSpec((n_sel, tg), lambda i: (0, i)), pl.BlockSpec((n_sel, tg), lambda i: (0, i))],
        out_specs=pl.BlockSpec((tg, PEER_NKEYS, PEER_NKEYS), lambda i: (i, 0, 0)),
        out_shape=jax.ShapeDtypeStruct((t, PEER_NKEYS, PEER_NKEYS), F32),
        scratch_shapes=[pltpu.VMEM((tg, n_sel), I32), pltpu.VMEM((tg, n_sel), F32)],
        compiler_params=_params(1),
        name="peer_gate_matrix",
    )(expert_t, gate_t)


def _rope_tables(s):
    inv_freq = 1.0 / (ROPE_THETA ** (np.arange(0, QK_ROPE, 2, dtype=np.float32) / QK_ROPE))
    ang = jnp.arange(s, dtype=F32)[:, None] * jnp.asarray(inv_freq, F32)[None, :]
    cos, sin = jnp.cos(ang), jnp.sin(ang)
    zeros = jnp.zeros((s, HEAD_DIM - QK_ROPE), F32)
    return (jnp.concatenate([cos, cos, zeros], axis=-1), jnp.concatenate([-sin, sin, zeros], axis=-1))


def _layer(x2, b, s, l, tables, g_mix, w_in, w_out, diff_lambda, diff_g_qk, diff_g_out, dil_g_qk, na_g_qk,
           na_rpb, mla_g_cq, mla_g_ckv, mla_w_uq, mla_w_ukv, mla_g_qk, g_ffn, peer_w_q, peer_sub_keys, peer_u,
           peer_v):
    t = b * s
    cos_t, sin_t, alibi, dil_tab, mult_tab = tables
    lambda_init = 0.8 - 0.6 * math.exp(-0.3 * l)
    n_qkv = 3 * QKV_COLS

    h = rms_rows(x2, g_mix)
    p = matmul(h, w_in[:, :n_qkv].astype(BF16)).reshape(b, s, n_qkv)
    w_lat = jnp.pad(w_in[:, n_qkv:], ((0, 0), (0, HEAD_DIM - QK_ROPE))).astype(BF16)
    p_lat = matmul(h, w_lat, tn=MLA_LATENT_PAD)
    c_q, c_kv, k_rope = mla_latent_norm(p_lat, mla_g_cq, mla_g_ckv)

    blocks = QKV_COLS // HEAD_DIM
    o_a = diff_attention(p, 0, diff_lambda, diff_g_qk, diff_g_out, lambda_init, alibi)
    o_b = dilated_attention(p, blocks, dil_g_qk, dil_tab, mult_tab)
    o_c = neighbourhood_attention(p, 2 * blocks, na_g_qk, na_rpb)

    w_uq = jnp.pad(mla_w_uq.reshape(Q_LORA, N_HEADS, MLA_QK), ((0, 0), (0, 0), (0, MLA_QK_PAD - MLA_QK)))
    q_up = matmul(c_q, w_uq.reshape(Q_LORA, N_HEADS * MLA_QK_PAD).astype(BF16), tm=1024)
    kv_up = matmul(c_kv, mla_w_ukv.astype(BF16), tm=1024)
    o_d = mla_attention(q_up.reshape(b, s, -1), kv_up.reshape(b, s, -1), k_rope.reshape(b, s, HEAD_DIM),
                        mla_g_qk, cos_t, sin_t)

    mixed = jnp.concatenate([o_a, o_b, o_c, o_d], axis=-1).reshape(t, 4 * GROUP_WIDTH)
    x2 = matmul(mixed, w_out.astype(BF16), residual=x2)

    hf = rms_rows(x2, g_ffn)
    q = matmul(hf, peer_w_q.astype(BF16))
    keys = peer_sub_keys.reshape(2 * PEER_HEADS, PEER_NKEYS, HEAD_DIM).astype(BF16)
    expert_t, gate_t = peer_route(q, keys)
    gates = peer_gate_matrix(expert_t, gate_t)
    act = peer_gated_act(hf, peer_u.astype(BF16), gates)
    return matmul_res_ktiled(act, peer_v.astype(BF16), x2)


def kernel(x, g_mix, w_in, w_out, diff_lambda, diff_g_qk, diff_g_out, dil_g_qk, na_g_qk, na_rpb, mla_g_cq,
           mla_g_ckv, mla_w_uq, mla_w_ukv, mla_g_qk, g_ffn, peer_w_q, peer_sub_keys, peer_u, peer_v):
    b, s, d = x.shape
    depth = g_mix.shape[0]
    tables = _rope_tables(s) + tuple(alibi_tables(s))
    x2 = x.reshape(b * s, d)
    per_layer = (g_mix, w_in, w_out, diff_lambda, diff_g_qk, diff_g_out, dil_g_qk, na_g_qk, na_rpb, mla_g_cq,
                 mla_g_ckv, mla_w_uq, mla_w_ukv, mla_g_qk, g_ffn, peer_w_q, peer_sub_keys, peer_u, peer_v)
    for l in range(depth):
        x2 = _layer(x2, b, s, l, tables, *(w[l] for w in per_layer))
    return x2.reshape(b, s, d)
```

```python
import functools
import math

import jax
import jax.numpy as jnp
import numpy as np
from jax import lax
from jax.experimental import pallas as pl
from jax.experimental.pallas import tpu as pltpu

F32 = jnp.float32
BF16 = jnp.bfloat16
I32 = jnp.int32

SUBLANES = 8
HEAD_DIM = 128
N_HEADS = 8
GROUP_WIDTH = N_HEADS * HEAD_DIM
QKV_COLS = 3 * GROUP_WIDTH
EPS = 1e-6
NEG_INF = -1e30
LOG2E = math.log2(math.e)
DIFF_DH = HEAD_DIM // 2
DIL_HALF = 64
DIL_DILATIONS = (1, 4, 16)
DIL_REACH = DIL_HALF * max(DIL_DILATIONS)
GRID_W = 64
NA_ROWS = 8
NA_COLS = 16
NA_Q_ROWS = 4
NA_K_ROWS = NA_Q_ROWS + NA_ROWS
Q_LORA = 768
KV_LORA = 512
QK_NOPE = 128
QK_ROPE = 64
MLA_QK = QK_NOPE + QK_ROPE
MLA_QK_PAD = 2 * HEAD_DIM
ROPE_THETA = 10000.0
PEER_HEADS = 8
PEER_NKEYS = 128
PEER_TOPK = 16
MLA_LATENT_PAD = Q_LORA + KV_LORA + HEAD_DIM
ATT_ROWS = 256
TAB_ROWS = 512

VMEM_LIMIT = 48 * 1024 * 1024

_NT = (((1,), (1,)), ((), ()))


def _params(grid_rank):
    return pltpu.CompilerParams(dimension_semantics=("arbitrary",) * grid_rank, vmem_limit_bytes=VMEM_LIMIT)


def _rms_rows_kernel(x_ref, g_ref, o_ref):
    x = x_ref[...]
    ms = jnp.mean(x * x, axis=-1, keepdims=True)
    o_ref[...] = (x * lax.rsqrt(ms + EPS) * g_ref[...]).astype(o_ref.dtype)


def rms_rows(x, g, tm=256):
    t, c = x.shape
    return pl.pallas_call(
        _rms_rows_kernel,
        grid=(t // tm,),
        in_specs=[pl.BlockSpec((tm, c), lambda i: (i, 0)), pl.BlockSpec((1, c), lambda i: (0, 0))],
        out_specs=pl.BlockSpec((tm, c), lambda i: (i, 0)),
        out_shape=jax.ShapeDtypeStruct((t, c), BF16),
        compiler_params=_params(1),
        name="rms_rows",
    )(x, g.reshape(1, c))


def _mla_latent_kernel(p_ref, gq_ref, gkv_ref, cq_ref, ckv_ref, kr_ref):
    cq = p_ref[:, :Q_LORA]
    ckv = p_ref[:, Q_LORA:Q_LORA + KV_LORA]
    cq_ref[...] = (cq * lax.rsqrt(jnp.mean(cq * cq, axis=-1, keepdims=True) + EPS) * gq_ref[...]).astype(BF16)
    ckv_ref[...] = (ckv * lax.rsqrt(jnp.mean(ckv * ckv, axis=-1, keepdims=True) + EPS) * gkv_ref[...]).astype(BF16)
    kr_ref[...] = p_ref[:, Q_LORA + KV_LORA:]


def mla_latent_norm(p_lat, g_cq, g_ckv, tm=512):
    t = p_lat.shape[0]
    return pl.pallas_call(
        _mla_latent_kernel,
        grid=(t // tm,),
        in_specs=[pl.BlockSpec((tm, MLA_LATENT_PAD), lambda i: (i, 0)),
                  pl.BlockSpec((1, Q_LORA), lambda i: (0, 0)),
                  pl.BlockSpec((1, KV_LORA), lambda i: (0, 0))],
        out_specs=[pl.BlockSpec((tm, Q_LORA), lambda i: (i, 0)),
                   pl.BlockSpec((tm, KV_LORA), lambda i: (i, 0)),
                   pl.BlockSpec((tm, HEAD_DIM), lambda i: (i, 0))],
        out_shape=[jax.ShapeDtypeStruct((t, Q_LORA), BF16),
                   jax.ShapeDtypeStruct((t, KV_LORA), BF16),
                   jax.ShapeDtypeStruct((t, HEAD_DIM), F32)],
        compiler_params=_params(1),
        name="mla_latent_norm",
    )(p_lat, g_cq.reshape(1, Q_LORA), g_ckv.reshape(1, KV_LORA))


def _mm_kernel(a_ref, b_ref, o_ref):
    o_ref[...] = jnp.dot(a_ref[...], b_ref[...], preferred_element_type=F32).astype(o_ref.dtype)


def _mm_res_kernel(a_ref, b_ref, r_ref, o_ref):
    o_ref[...] = r_ref[...] + jnp.dot(a_ref[...], b_ref[...], preferred_element_type=F32)


def _mm_res_acc_kernel(a_ref, b_ref, r_ref, o_ref):
    @pl.when(pl.program_id(2) == 0)
    def _():
        o_ref[...] = r_ref[...]

    o_ref[...] += jnp.dot(a_ref[...], b_ref[...], preferred_element_type=F32)


def matmul(a, b, *, out_dtype=F32, residual=None, tm=512, tn=1024):
    m, kd = a.shape
    n = b.shape[1]
    tm, tn = min(tm, m), min(tn, n)
    in_specs = [pl.BlockSpec((tm, kd), lambda j, i: (i, 0)),
                pl.BlockSpec((kd, tn), lambda j, i: (0, j))]
    args = [a, b]
    body = _mm_kernel
    if residual is not None:
        body = _mm_res_kernel
        in_specs.append(pl.BlockSpec((tm, tn), lambda j, i: (i, j)))
        args.append(residual)
    return pl.pallas_call(
        body,
        grid=(n // tn, m // tm),
        in_specs=in_specs,
        out_specs=pl.BlockSpec((tm, tn), lambda j, i: (i, j)),
        out_shape=jax.ShapeDtypeStruct((m, n), out_dtype),
        compiler_params=_params(2),
        name="matmul_res" if residual is not None else "matmul",
    )(*args)


def matmul_res_ktiled(a, b, residual, tm=1024, tn=1024, tk=2048):
    m, kd = a.shape
    n = b.shape[1]
    return pl.pallas_call(
        _mm_res_acc_kernel,
        grid=(m // tm, n // tn, kd // tk),
        in_specs=[pl.BlockSpec((tm, tk), lambda i, j, k: (i, k)),
                  pl.BlockSpec((tk, tn), lambda i, j, k: (k, j)),
                  pl.BlockSpec((tm, tn), lambda i, j, k: (i, j))],
        out_specs=pl.BlockSpec((tm, tn), lambda i, j, k: (i, j)),
        out_shape=jax.ShapeDtypeStruct((m, n), F32),
        compiler_params=_params(3),
        name="matmul_res_ktiled",
    )(a, b, residual)


def _gelu_exact(x):
    return 0.5 * x * (1.0 + lax.erf(x * (1.0 / math.sqrt(2.0))))


def _peer_act_kernel(h_ref, u_ref, g_ref, o_ref):
    tm, n_key0, n_key1 = g_ref.shape
    g = jnp.swapaxes(g_ref[...].reshape(tm // SUBLANES, SUBLANES, n_key0, n_key1), 1, 2)
    act = _gelu_exact(lax.dot_general(h_ref[...], u_ref[...], _NT, preferred_element_type=F32))
    for a in range(n_key0):
        cols = slice(a * n_key1, (a + 1) * n_key1)
        o_ref[:, cols] = (act[:, cols] * g[:, a].reshape(tm, n_key1)).astype(o_ref.dtype)


def peer_gated_act(hn, u, gates, tm=512, tn=1024):
    t, d = hn.shape
    e = u.shape[0]
    tm = min(tm, t)
    n_key0 = tn // PEER_NKEYS
    return pl.pallas_call(
        _peer_act_kernel,
        grid=(e // tn, t // tm),
        in_specs=[pl.BlockSpec((tm, d), lambda j, i: (i, 0)),
                  pl.BlockSpec((tn, d), lambda j, i: (j, 0)),
                  pl.BlockSpec((tm, n_key0, PEER_NKEYS), lambda j, i: (i, j, 0))],
        out_specs=pl.BlockSpec((tm, tn), lambda j, i: (i, j)),
        out_shape=jax.ShapeDtypeStruct((t, e), BF16),
        compiler_params=_params(2),
        name="peer_gated_act",
    )(hn, u, gates)


def _rms_lanes(x, g):
    ms = jnp.mean(x * x, axis=-1, keepdims=True)
    return x * lax.rsqrt(ms + EPS) * g


def _rms_halves(x, g):
    lane = lax.broadcasted_iota(I32, x.shape, 1)
    lo = lane < DIFF_DH
    xx = x * x
    ms_lo = jnp.sum(jnp.where(lo, xx, 0.0), axis=-1, keepdims=True) * (1.0 / DIFF_DH)
    ms_hi = jnp.sum(jnp.where(lo, 0.0, xx), axis=-1, keepdims=True) * (1.0 / DIFF_DH)
    inv = jnp.where(lo, lax.rsqrt(ms_lo + EPS), lax.rsqrt(ms_hi + EPS))
    return x * inv * g


def _exp2_weights(s2):
    return jnp.exp2(s2 - jnp.max(s2, axis=-1, keepdims=True))


def _values_with_ones(v):
    ones_col = jnp.where(lax.broadcasted_iota(I32, v.shape, 1) == 0, 1.0, 0.0)
    return jnp.concatenate([v, ones_col], axis=-1).astype(BF16)


def _weighted_values(p, vx):
    ov = jnp.dot(p.astype(BF16), vx, preferred_element_type=F32)
    return ov[:, :HEAD_DIM], ov[:, HEAD_DIM:HEAD_DIM + 1]


def _alibi_tables_kernel(slope_ref, alibi_ref, dil_ref, mult_ref, *, s_len):
    shape = alibi_ref.shape
    d = lax.broadcasted_iota(I32, shape, 1) - lax.broadcasted_iota(I32, shape, 0) - (s_len - TAB_ROWS)
    ad = jnp.abs(d)
    alibi = (-LOG2E * slope_ref[pl.program_id(0)]) * ad.astype(F32)
    mult = jnp.zeros(shape, F32)
    for dil in DIL_DILATIONS:
        mult = mult + jnp.where(((d & (dil - 1)) == 0) & (ad <= DIL_HALF * dil), 1.0, 0.0)
    alibi_ref[...] = alibi
    dil_ref[...] = jnp.where(mult > 0.0, alibi, NEG_INF)
    mult_ref[...] = mult


def alibi_tables(s_len):
    slopes = jnp.asarray(2.0 ** (-8.0 * np.arange(1, N_HEADS + 1) / N_HEADS), F32)
    width = 2 * s_len - TAB_ROWS
    per_head = pl.BlockSpec((None, TAB_ROWS, width), lambda h: (h, 0, 0))
    return pl.pallas_call(
        functools.partial(_alibi_tables_kernel, s_len=s_len),
        grid=(N_HEADS,),
        in_specs=[pl.BlockSpec(memory_space=pltpu.SMEM)],
        out_specs=[per_head, per_head, pl.BlockSpec((TAB_ROWS, width), lambda h: (0, 0))],
        out_shape=[jax.ShapeDtypeStruct((N_HEADS, TAB_ROWS, width), F32),
                   jax.ShapeDtypeStruct((N_HEADS, TAB_ROWS, width), F32),
                   jax.ShapeDtypeStruct((TAB_ROWS, width), F32)],
        compiler_params=_params(1),
        name="alibi_tables",
    )(slopes)


def _table_tile(tab_ref, r0, k_lo, k_hi, s_len):
    row = r0 % TAB_ROWS
    lane0 = s_len - TAB_ROWS - (r0 - row) + k_lo
    return tab_ref[row:row + ATT_ROWS, lane0:lane0 + (k_hi - k_lo)]


def _head_specs(s, col0, width=HEAD_DIM):
    return [pl.BlockSpec((None, s, width), lambda h, bi, c=c: (bi, 0, col0 + c * N_HEADS + h)) for c in range(3)]


def _diff_attn_kernel(q_ref, k_ref, v_ref, gq_ref, gk_ref, go_ref, lam_ref, tab_ref, o_ref, kn_ref, vb_ref,
                      *, lambda_init):
    s_len = k_ref.shape[0]
    kn_ref[...] = _rms_halves(k_ref[...], gk_ref[...]).astype(BF16)
    vb_ref[...] = _values_with_ones(v_ref[...])
    lv = lam_ref[...]
    lam = (jnp.exp(jnp.sum(lv[0:1] * lv[1:2], axis=-1, keepdims=True))
           - jnp.exp(jnp.sum(lv[2:3] * lv[3:4], axis=-1, keepdims=True)) + lambda_init)
    lo = lax.broadcasted_iota(I32, (ATT_ROWS, HEAD_DIM), 1) < DIFF_DH
    for r0 in range(0, s_len, ATT_ROWS):
        qn = _rms_halves(q_ref[r0:r0 + ATT_ROWS, :], gq_ref[...]) * (DIFF_DH ** -0.5 * LOG2E)
        tab = _table_tile(tab_ref, r0, 0, s_len, s_len)
        p0 = _exp2_weights(lax.dot_general(jnp.where(lo, qn, 0.0).astype(BF16), kn_ref[...], _NT,
                                           preferred_element_type=F32) + tab)
        p1 = _exp2_weights(lax.dot_general(jnp.where(lo, 0.0, qn).astype(BF16), kn_ref[...], _NT,
                                           preferred_element_type=F32) + tab)
        o0, l0 = _weighted_values(p0, vb_ref[...])
        o1, l1 = _weighted_values(p1, vb_ref[...])
        o = o0 * (1.0 / l0) - o1 * (lam / l1)
        o_ref[r0:r0 + ATT_ROWS, :] = (_rms_lanes(o, go_ref[...]) * (1.0 - lambda_init)).astype(o_ref.dtype)


def diff_attention(p, col0, lam_vecs, g_qk, g_out, lambda_init, alibi):
    b, s, _ = p.shape
    lam_pad = jnp.pad(lam_vecs, ((0, 0), (0, HEAD_DIM - DIFF_DH)))
    gq = jnp.tile(g_qk[0], 2).reshape(1, HEAD_DIM)
    gk = jnp.tile(g_qk[1], 2).reshape(1, HEAD_DIM)
    vec = pl.BlockSpec((1, HEAD_DIM), lambda h, bi: (0, 0))
    return pl.pallas_call(
        functools.partial(_diff_attn_kernel, lambda_init=lambda_init),
        grid=(N_HEADS, b),
        in_specs=_head_specs(s, col0) + [
            vec, vec, vec,
            pl.BlockSpec((4, HEAD_DIM), lambda h, bi: (0, 0)),
            pl.BlockSpec((None,) + alibi.shape[1:], lambda h, bi: (h, 0, 0))],
        out_specs=pl.BlockSpec((None, s, HEAD_DIM), lambda h, bi: (bi, 0, h)),
        out_shape=jax.ShapeDtypeStruct((b, s, GROUP_WIDTH), BF16),
        scratch_shapes=[pltpu.VMEM((s, HEAD_DIM), BF16), pltpu.VMEM((s, 2 * HEAD_DIM), BF16)],
        compiler_params=_params(2),
        name="diff_attention",
    )(p, p, p, gq, gk, g_out.reshape(1, HEAD_DIM), lam_pad, alibi)


def _dilated_attn_kernel(q_ref, k_ref, v_ref, gq_ref, gk_ref, tab_ref, mult_ref, o_ref, kn_ref, vb_ref):
    s_len = k_ref.shape[0]
    kn_ref[...] = _rms_lanes(k_ref[...], gk_ref[...]).astype(BF16)
    vb_ref[...] = _values_with_ones(v_ref[...])
    for r0 in range(0, s_len, ATT_ROWS):
        k_lo, k_hi = max(0, r0 - DIL_REACH), min(s_len, r0 + ATT_ROWS + DIL_REACH)
        qn = (_rms_lanes(q_ref[r0:r0 + ATT_ROWS, :], gq_ref[...]) * (HEAD_DIM ** -0.5 * LOG2E)).astype(BF16)
        s2 = (lax.dot_general(qn, kn_ref[k_lo:k_hi, :], _NT, preferred_element_type=F32)
              + _table_tile(tab_ref, r0, k_lo, k_hi, s_len))
        o, l = _weighted_values(_exp2_weights(s2) * _table_tile(mult_ref, r0, k_lo, k_hi, s_len),
                                vb_ref[k_lo:k_hi, :])
        o_ref[r0:r0 + ATT_ROWS, :] = (o * (1.0 / l)).astype(o_ref.dtype)


def dilated_attention(p, col0, g_qk, dil_tab, mult_tab):
    b, s, _ = p.shape
    vec = pl.BlockSpec((1, HEAD_DIM), lambda h, bi: (0, 0))
    return pl.pallas_call(
        _dilated_attn_kernel,
        grid=(N_HEADS, b),
        in_specs=_head_specs(s, col0) + [
            vec, vec,
            pl.BlockSpec((None,) + dil_tab.shape[1:], lambda h, bi: (h, 0, 0)),
            pl.BlockSpec(mult_tab.shape, lambda h, bi: (0, 0))],
        out_specs=pl.BlockSpec((None, s, HEAD_DIM), lambda h, bi: (bi, 0, h)),
        out_shape=jax.ShapeDtypeStruct((b, s, GROUP_WIDTH), BF16),
        scratch_shapes=[pltpu.VMEM((s, HEAD_DIM), BF16), pltpu.VMEM((s, 2 * HEAD_DIM), BF16)],
        compiler_params=_params(2),
        name="dilated_attention",
    )(p, p, p, g_qk[0].reshape(1, HEAD_DIM), g_qk[1].reshape(1, HEAD_DIM), dil_tab, mult_tab)


def _na_window_start(q_row0, n_rows):
    return min(max(q_row0 - NA_ROWS // 2, 0), n_rows - NA_K_ROWS)


def _na_attn_kernel(q_ref, k_ref, v_ref, gq_ref, gk_ref, bias_ref, o_ref, kn_ref, vb_ref):
    n_rows = k_ref.shape[0] // GRID_W
    kn_ref[...] = _rms_lanes(k_ref[...], gk_ref[...]).astype(BF16)
    vb_ref[...] = _values_with_ones(v_ref[...])
    tq, tk = NA_Q_ROWS * GRID_W, NA_K_ROWS * GRID_W
    for qb in range(n_rows // NA_Q_ROWS):
        k0 = _na_window_start(qb * NA_Q_ROWS, n_rows) * GRID_W
        qn = (_rms_lanes(q_ref[qb * tq:(qb + 1) * tq, :], gq_ref[...]) * (HEAD_DIM ** -0.5 * LOG2E)).astype(BF16)
        p = _exp2_weights(lax.dot_general(qn, kn_ref[k0:k0 + tk, :], _NT, preferred_element_type=F32)
                          + bias_ref[qb])
        o, l = _weighted_values(p, vb_ref[k0:k0 + tk, :])
        o_ref[qb * tq:(qb + 1) * tq, :] = (o * (1.0 / l)).astype(o_ref.dtype)


def _na_bias_kernel(t_ref, o_ref, *, n_rows):
    lane = lax.broadcasted_iota(I32, (GRID_W, 2 * GRID_W), 1)
    for qb in range(n_rows // NA_Q_ROWS):
        q_row0 = qb * NA_Q_ROWS
        k_row0 = _na_window_start(q_row0, n_rows)
        for a in range(NA_Q_ROWS):
            rq = q_row0 + a
            rs = min(max(rq - NA_ROWS // 2, 0), n_rows - NA_ROWS)
            for wp in range(NA_K_ROWS // 2):
                rk = k_row0 + 2 * wp
                ok0, ok1 = rs <= rk < rs + NA_ROWS, rs <= rk + 1 < rs + NA_ROWS
                m = min(max(rk - rq + NA_ROWS, 0), 2 * NA_ROWS - 1)
                tile = t_ref[m] * LOG2E
                if not (ok0 and ok1):
                    keep = (lane < GRID_W) if ok0 else (lane >= GRID_W)
                    tile = jnp.where(keep, tile, NEG_INF) if (ok0 or ok1) else jnp.full_like(tile, NEG_INF)
                o_ref[qb, a * GRID_W:(a + 1) * GRID_W, wp * 2 * GRID_W:(wp + 1) * 2 * GRID_W] = tile


def _na_bias(rpb, n_rows):
    n_rel_r = 2 * NA_ROWS - 1
    pad = GRID_W - NA_COLS
    rp = jnp.pad(rpb, ((0, 0), (0, 0), (pad, pad)))
    toe = jnp.stack([rp[:, :, NA_COLS - 1 + pad - cq:NA_COLS - 1 + pad - cq + GRID_W] for cq in range(GRID_W)],
                    axis=2)
    cq = np.arange(GRID_W)
    cs = np.clip(cq - NA_COLS // 2, 0, GRID_W - NA_COLS)
    col_ok = (cq[None, :] >= cs[:, None]) & (cq[None, :] < cs[:, None] + NA_COLS)
    toe = jnp.where(col_ok[None, None], toe, NEG_INF)
    ext = jnp.pad(toe, ((0, 0), (1, 1), (0, 0), (0, 0)), constant_values=NEG_INF)
    pairs = jnp.concatenate([ext[:, :n_rel_r + 1], ext[:, 1:]], axis=-1)
    tq, tk = NA_Q_ROWS * GRID_W, NA_K_ROWS * GRID_W
    nqb = n_rows // NA_Q_ROWS
    return pl.pallas_call(
        functools.partial(_na_bias_kernel, n_rows=n_rows),
        grid=(N_HEADS,),
        in_specs=[pl.BlockSpec((None, n_rel_r + 1, GRID_W, 2 * GRID_W), lambda h: (h, 0, 0, 0))],
        out_specs=pl.BlockSpec((None, nqb, tq, tk), lambda h: (h, 0, 0, 0)),
        out_shape=jax.ShapeDtypeStruct((N_HEADS, nqb, tq, tk), F32),
        compiler_params=_params(1),
        name="na_bias",
    )(pairs)


def neighbourhood_attention(p, col0, g_qk, rpb):
    b, s, _ = p.shape
    bias = _na_bias(rpb, s // GRID_W)
    vec = pl.BlockSpec((1, HEAD_DIM), lambda h, bi: (0, 0))
    return pl.pallas_call(
        _na_attn_kernel,
        grid=(N_HEADS, b),
        in_specs=_head_specs(s, col0) + [
            vec, vec,
            pl.BlockSpec((None,) + bias.shape[1:], lambda h, bi: (h, 0, 0, 0))],
        out_specs=pl.BlockSpec((None, s, HEAD_DIM), lambda h, bi: (bi, 0, h)),
        out_shape=jax.ShapeDtypeStruct((b, s, GROUP_WIDTH), BF16),
        scratch_shapes=[pltpu.VMEM((s, HEAD_DIM), BF16), pltpu.VMEM((s, 2 * HEAD_DIM), BF16)],
        compiler_params=_params(2),
        name="neighbourhood_attention",
    )(p, p, p, g_qk[0].reshape(1, HEAD_DIM), g_qk[1].reshape(1, HEAD_DIM), bias)


def _mla_norm_rope(x, g, cos, sin):
    ms = jnp.sum(x * x, axis=-1, keepdims=True) * (1.0 / MLA_QK)
    xn = x * lax.rsqrt(ms + EPS) * g
    xr = xn[:, HEAD_DIM:]
    partner = pltpu.roll(xr, QK_ROPE // 2, 1) + pltpu.roll(xr, HEAD_DIM - QK_ROPE // 2, 1)
    return jnp.concatenate([xn[:, :HEAD_DIM], xr * cos + partner * sin], axis=-1)


def _mla_attn_kernel(q_ref, kn_in_ref, v_ref, kr_ref, cos_ref, sin_ref, gq_ref, gk_ref, o_ref, kf_ref, vb_ref):
    s_len = v_ref.shape[0]
    k = jnp.concatenate([kn_in_ref[...], kr_ref[...]], axis=-1)
    kf_ref[...] = _mla_norm_rope(k, gk_ref[...], cos_ref[...], sin_ref[...]).astype(BF16)
    vb_ref[...] = _values_with_ones(v_ref[...])
    def scores(r0):
        rows = slice(r0, r0 + ATT_ROWS)
        qf = (_mla_norm_rope(q_ref[rows, :], gq_ref[...], cos_ref[rows, :], sin_ref[rows, :])
              * (MLA_QK ** -0.5 * LOG2E)).astype(BF16)
        return lax.dot_general(qf, kf_ref[...], _NT, preferred_element_type=F32)

    s2_next = scores(0)
    for r0 in range(0, s_len, ATT_ROWS):
        s2 = s2_next
        if r0 + ATT_ROWS < s_len:
            s2_next = scores(r0 + ATT_ROWS)
        o, l = _weighted_values(_exp2_weights(s2), vb_ref[...])
        o_ref[r0:r0 + ATT_ROWS, :] = (o * (1.0 / l)).astype(o_ref.dtype)


def mla_attention(q_up, kv_up, k_rope, g_qk, cos_t, sin_t):
    b, s, _ = q_up.shape
    pad = MLA_QK_PAD - MLA_QK
    gq = jnp.pad(g_qk[0], (0, pad)).reshape(1, MLA_QK_PAD)
    gk = jnp.pad(g_qk[1], (0, pad)).reshape(1, MLA_QK_PAD)
    vec = pl.BlockSpec((1, MLA_QK_PAD), lambda bi, h: (0, 0))
    rope = pl.BlockSpec((s, HEAD_DIM), lambda bi, h: (0, 0))
    return pl.pallas_call(
        _mla_attn_kernel,
        grid=(b, N_HEADS),
        in_specs=[pl.BlockSpec((None, s, MLA_QK_PAD), lambda bi, h: (bi, 0, h)),
                  pl.BlockSpec((None, s, HEAD_DIM), lambda bi, h: (bi, 0, 2 * h)),
                  pl.BlockSpec((None, s, HEAD_DIM), lambda bi, h: (bi, 0, 2 * h + 1)),
                  pl.BlockSpec((None, s, HEAD_DIM), lambda bi, h: (bi, 0, 0)),
                  rope, rope, vec, vec],
        out_specs=pl.BlockSpec((None, s, HEAD_DIM), lambda bi, h: (bi, 0, h)),
        out_shape=jax.ShapeDtypeStruct((b, s, GROUP_WIDTH), BF16),
        scratch_shapes=[pltpu.VMEM((s, MLA_QK_PAD), BF16), pltpu.VMEM((s, 2 * HEAD_DIM), BF16)],
        compiler_params=_params(2),
        name="mla_attention",
    )(q_up, kv_up, kv_up, k_rope, cos_t, sin_t, gq, gk)


def _top_k_lead(s, order, payload=None):
    vals, picks = [], []
    for _ in range(PEER_TOPK):
        m = jnp.max(s, axis=0, keepdims=True)
        first = jnp.min(jnp.where(s == m, order, jnp.iinfo(jnp.int32).max), axis=0, keepdims=True)
        sel = order == first
        vals.append(m)
        picks.append(first if payload is None else jnp.max(jnp.where(sel, payload, 0), axis=0, keepdims=True))
        s = jnp.where(sel, -jnp.inf, s)
    return jnp.concatenate(vals, axis=0), jnp.concatenate(picks, axis=0)


_PAIRS = tuple((i, j) for i in range(PEER_TOPK) for j in range(PEER_TOPK) if (i + 1) * (j + 1) <= PEER_TOPK)


def _peer_route_kernel(q_ref, keys_ref, expert_ref, gate_ref):
    groups = q_ref.shape[0] // HEAD_DIM
    tile = (1, groups, HEAD_DIM)
    key_id = lax.broadcasted_iota(I32, (PEER_NKEYS,) + tile[1:], 0)
    cand_flat = jnp.concatenate([jnp.full(tile, i * PEER_TOPK + j, I32) for i, j in _PAIRS], axis=0)
    tops = []
    for hc in range(keys_ref.shape[0]):
        cols = slice(hc * HEAD_DIM, (hc + 1) * HEAD_DIM)
        sc = jnp.stack([lax.dot_general(keys_ref[hc], q_ref[g * HEAD_DIM:(g + 1) * HEAD_DIM, cols].astype(BF16),
                                        _NT, preferred_element_type=F32) for g in range(groups)], axis=0)
        tops.append(_top_k_lead(jnp.swapaxes(sc, 0, 1), key_id))
    for hh in range(keys_ref.shape[0] // 2):
        (s0, i0), (s1, i1) = tops[2 * hh], tops[2 * hh + 1]
        cand_s = jnp.concatenate([s0[i:i + 1] + s1[j:j + 1] for i, j in _PAIRS], axis=0)
        cand_e = jnp.concatenate([i0[i:i + 1] * PEER_NKEYS + i1[j:j + 1] for i, j in _PAIRS], axis=0)
        best_s, best_e = _top_k_lead(cand_s, cand_flat, cand_e)
        e = jnp.exp(best_s - jnp.max(best_s, axis=0, keepdims=True))
        gate = jnp.swapaxes(e * (1.0 / jnp.sum(e, axis=0, keepdims=True)), 0, 1)
        expert = jnp.swapaxes(best_e, 0, 1)
        rows = slice(hh * PEER_TOPK, (hh + 1) * PEER_TOPK)
        for g in range(groups):
            gate_ref[rows, g * HEAD_DIM:(g + 1) * HEAD_DIM] = gate[g]
            expert_ref[rows, g * HEAD_DIM:(g + 1) * HEAD_DIM] = expert[g]


def peer_route(q, sub_keys, tt=1024, heads_per_step=2):
    t = q.shape[0]
    n_sel = PEER_HEADS * PEER_TOPK
    hs = heads_per_step
    return pl.pallas_call(
        _peer_route_kernel,
        grid=(t // tt, PEER_HEADS // hs),
        in_specs=[pl.BlockSpec((tt, 2 * hs * HEAD_DIM), lambda i, h: (i, h)),
                  pl.BlockSpec((2 * hs, PEER_NKEYS, HEAD_DIM), lambda i, h: (h, 0, 0))],
        out_specs=[pl.BlockSpec((hs * PEER_TOPK, tt), lambda i, h: (h, i)),
                   pl.BlockSpec((hs * PEER_TOPK, tt), lambda i, h: (h, i))],
        out_shape=[jax.ShapeDtypeStruct((n_sel, t), I32), jax.ShapeDtypeStruct((n_sel, t), F32)],
        compiler_params=_params(2),
        name="peer_route",
    )(q, sub_keys)


def _peer_gate_matrix_kernel(expert_ref, gate_ref, o_ref, e_s, g_s):
    e_s[...] = expert_ref[...].T
    g_s[...] = gate_ref[...].T
    n_sel = e_s.shape[1]
    row = lax.broadcasted_iota(I32, (PEER_NKEYS, n_sel), 0)

    def body(t, carry):
        e_row = e_s[pl.ds(t, 1), :]
        g_row = g_s[pl.ds(t, 1), :]
        hi = jnp.where(row == (e_row >> 7), g_row, 0.0).astype(BF16)
        lo = jnp.where(row == (e_row & (PEER_NKEYS - 1)), 1.0, 0.0).astype(BF16)
        o_ref[t] = lax.dot_general(hi, lo, _NT, preferred_element_type=F32)
        return carry

    lax.fori_loop(0, e_s.shape[0], body, 0, unroll=16)


def peer_gate_matrix(expert_t, gate_t, tg=128):
    n_sel, t = expert_t.shape
    return pl.pallas_call(
        _peer_gate_matrix_kernel,
        grid=(t // tg,),
        in_specs=[pl.BlockSpec((n_sel, tg), lambda i: (0, i)), pl.BlockSpec((n_sel, tg), lambda i: (0, i))],
        out_specs=pl.BlockSpec((tg, PEER_NKEYS, PEER_NKEYS), lambda i: (i, 0, 0)),
        out_shape=jax.ShapeDtypeStruct((t, PEER_NKEYS, PEER_NKEYS), F32),
        scratch_shapes=[pltpu.VMEM((tg, n_sel), I32), pltpu.VMEM((tg, n_sel), F32)],
        compiler_params=_params(1),
        name="peer_gate_matrix",
    )(expert_t, gate_t)


def _rope_tables(s):
    inv_freq = 1.0 / (ROPE_THETA ** (np.arange(0, QK_ROPE, 2, dtype=np.float32) / QK_ROPE))
    ang = jnp.arange(s, dtype=F32)[:, None] * jnp.asarray(inv_freq, F32)[None, :]
    cos, sin = jnp.cos(ang), jnp.sin(ang)
    zeros = jnp.zeros((s, HEAD_DIM - QK_ROPE), F32)
    return (jnp.concatenate([cos, cos, zeros], axis=-1), jnp.concatenate([-sin, sin, zeros], axis=-1))


def _layer(x2, b, s, l, tables, g_mix, w_in, w_out, diff_lambda, diff_g_qk, diff_g_out, dil_g_qk, na_g_qk,
           na_rpb, mla_g_cq, mla_g_ckv, mla_w_uq, mla_w_ukv, mla_g_qk, g_ffn, peer_w_q, peer_sub_keys, peer_u,
           peer_v):
    t = b * s
    cos_t, sin_t, alibi, dil_tab, mult_tab = tables
    lambda_init = 0.8 - 0.6 * math.exp(-0.3 * l)
    n_qkv = 3 * QKV_COLS

    h = rms_rows(x2, g_mix)
    p = matmul(h, w_in[:, :n_qkv].astype(BF16), tm=1024).reshape(b, s, n_qkv)
    w_lat = jnp.pad(w_in[:, n_qkv:], ((0, 0), (0, HEAD_DIM - QK_ROPE))).astype(BF16)
    p_lat = matmul(h, w_lat, tn=MLA_LATENT_PAD)
    c_q, c_kv, k_rope = mla_latent_norm(p_lat, mla_g_cq, mla_g_ckv)

    blocks = QKV_COLS // HEAD_DIM
    o_a = diff_attention(p, 0, diff_lambda, diff_g_qk, diff_g_out, lambda_init, alibi)
    o_b = dilated_attention(p, blocks, dil_g_qk, dil_tab, mult_tab)
    o_c = neighbourhood_attention(p, 2 * blocks, na_g_qk, na_rpb)

    w_uq = jnp.pad(mla_w_uq.reshape(Q_LORA, N_HEADS, MLA_QK), ((0, 0), (0, 0), (0, MLA_QK_PAD - MLA_QK)))
    q_up = matmul(c_q, w_uq.reshape(Q_LORA, N_HEADS * MLA_QK_PAD).astype(BF16), tm=1024)
    kv_up = matmul(c_kv, mla_w_ukv.astype(BF16), tm=1024)
    o_d = mla_attention(q_up.reshape(b, s, -1), kv_up.reshape(b, s, -1), k_rope.reshape(b, s, HEAD_DIM),
                        mla_g_qk, cos_t, sin_t)

    mixed = jnp.concatenate([o_a, o_b, o_c, o_d], axis=-1).reshape(t, 4 * GROUP_WIDTH)
    x2 = matmul(mixed, w_out.astype(BF16), residual=x2)

    hf = rms_rows(x2, g_ffn)
    q = matmul(hf, peer_w_q.astype(BF16), tm=1024)
    keys = peer_sub_keys.reshape(2 * PEER_HEADS, PEER_NKEYS, HEAD_DIM).astype(BF16)
    expert_t, gate_t = peer_route(q, keys)
    gates = peer_gate_matrix(expert_t, gate_t)
    act = peer_gated_act(hf, peer_u.astype(BF16), gates)
    return matmul_res_ktiled(act, peer_v.astype(BF16), x2)


def kernel(x, g_mix, w_in, w_out, diff_lambda, diff_g_qk, diff_g_out, dil_g_qk, na_g_qk, na_rpb, mla_g_cq,
           mla_g_ckv, mla_w_uq, mla_w_ukv, mla_g_qk, g_ffn, peer_w_q, peer_sub_keys, peer_u, peer_v):
    b, s, d = x.shape
    depth = g_mix.shape[0]
    tables = _rope_tables(s) + tuple(alibi_tables(s))
    x2 = x.reshape(b * s, d)
    per_layer = (g_mix, w_in, w_out, diff_lambda, diff_g_qk, diff_g_out, dil_g_qk, na_g_qk, na_rpb, mla_g_cq,
                 mla_g_ckv, mla_w_uq, mla_w_ukv, mla_g_qk, g_ffn, peer_w_q, peer_sub_keys, peer_u, peer_v)
    for l in range(depth):
        x2 = _layer(x2, b, s, l, tables, *(w[l] for w in per_layer))
    return x2.reshape(b, s, d)
```

```python
import functools
import math

import jax
import jax.numpy as jnp
import numpy as np
from jax import lax
from jax.experimental import pallas as pl
from jax.experimental.pallas import tpu as pltpu

F32 = jnp.float32
BF16 = jnp.bfloat16
I32 = jnp.int32

SUBLANES = 8
HEAD_DIM = 128
N_HEADS = 8
GROUP_WIDTH = N_HEADS * HEAD_DIM
QKV_COLS = 3 * GROUP_WIDTH
EPS = 1e-6
NEG_INF = -1e30
LOG2E = math.log2(math.e)
DIFF_DH = HEAD_DIM // 2
DIL_HALF = 64
DIL_DILATIONS = (1, 4, 16)
DIL_REACH = DIL_HALF * max(DIL_DILATIONS)
GRID_W = 64
NA_ROWS = 8
NA_COLS = 16
NA_Q_ROWS = 4
NA_K_ROWS = NA_Q_ROWS + NA_ROWS
Q_LORA = 768
KV_LORA = 512
QK_NOPE = 128
QK_ROPE = 64
MLA_QK = QK_NOPE + QK_ROPE
MLA_QK_PAD = 2 * HEAD_DIM
ROPE_THETA = 10000.0
PEER_HEADS = 8
PEER_NKEYS = 128
PEER_TOPK = 16
MLA_LATENT_PAD = Q_LORA + KV_LORA + HEAD_DIM
ATT_ROWS = 256
TAB_ROWS = 512

VMEM_LIMIT = 54 * 1024 * 1024

_NT = (((1,), (1,)), ((), ()))


def _params(grid_rank):
    return pltpu.CompilerParams(dimension_semantics=("arbitrary",) * grid_rank, vmem_limit_bytes=VMEM_LIMIT)


def _rms_rows_kernel(x_ref, g_ref, o_ref):
    x = x_ref[...]
    ms = jnp.mean(x * x, axis=-1, keepdims=True)
    o_ref[...] = (x * lax.rsqrt(ms + EPS) * g_ref[...]).astype(o_ref.dtype)


def rms_rows(x, g, tm=256):
    t, c = x.shape
    return pl.pallas_call(
        _rms_rows_kernel,
        grid=(t // tm,),
        in_specs=[pl.BlockSpec((tm, c), lambda i: (i, 0)), pl.BlockSpec((1, c), lambda i: (0, 0))],
        out_specs=pl.BlockSpec((tm, c), lambda i: (i, 0)),
        out_shape=jax.ShapeDtypeStruct((t, c), BF16),
        compiler_params=_params(1),
        name="rms_rows",
    )(x, g.reshape(1, c))


def _mla_latent_kernel(p_ref, gq_ref, gkv_ref, cq_ref, ckv_ref, kr_ref):
    cq = p_ref[:, :Q_LORA]
    ckv = p_ref[:, Q_LORA:Q_LORA + KV_LORA]
    cq_ref[...] = (cq * lax.rsqrt(jnp.mean(cq * cq, axis=-1, keepdims=True) + EPS) * gq_ref[...]).astype(BF16)
    ckv_ref[...] = (ckv * lax.rsqrt(jnp.mean(ckv * ckv, axis=-1, keepdims=True) + EPS) * gkv_ref[...]).astype(BF16)
    kr_ref[...] = p_ref[:, Q_LORA + KV_LORA:]


def mla_latent_norm(p_lat, g_cq, g_ckv, tm=512):
    t = p_lat.shape[0]
    return pl.pallas_call(
        _mla_latent_kernel,
        grid=(t // tm,),
        in_specs=[pl.BlockSpec((tm, MLA_LATENT_PAD), lambda i: (i, 0)),
                  pl.BlockSpec((1, Q_LORA), lambda i: (0, 0)),
                  pl.BlockSpec((1, KV_LORA), lambda i: (0, 0))],
        out_specs=[pl.BlockSpec((tm, Q_LORA), lambda i: (i, 0)),
                   pl.BlockSpec((tm, KV_LORA), lambda i: (i, 0)),
                   pl.BlockSpec((tm, HEAD_DIM), lambda i: (i, 0))],
        out_shape=[jax.ShapeDtypeStruct((t, Q_LORA), BF16),
                   jax.ShapeDtypeStruct((t, KV_LORA), BF16),
                   jax.ShapeDtypeStruct((t, HEAD_DIM), F32)],
        compiler_params=_params(1),
        name="mla_latent_norm",
    )(p_lat, g_cq.reshape(1, Q_LORA), g_ckv.reshape(1, KV_LORA))


def _mm_kernel(a_ref, b_ref, o_ref):
    o_ref[...] = jnp.dot(a_ref[...], b_ref[...], preferred_element_type=F32).astype(o_ref.dtype)


def _mm_res_kernel(a_ref, b_ref, r_ref, o_ref):
    o_ref[...] = r_ref[...] + jnp.dot(a_ref[...], b_ref[...], preferred_element_type=F32)


def _mm_res_acc_kernel(a_ref, b_ref, r_ref, o_ref):
    @pl.when(pl.program_id(2) == 0)
    def _():
        o_ref[...] = r_ref[...]

    o_ref[...] += jnp.dot(a_ref[...], b_ref[...], preferred_element_type=F32)


def matmul(a, b, *, out_dtype=F32, residual=None, tm=512, tn=1024):
    m, kd = a.shape
    n = b.shape[1]
    tm, tn = min(tm, m), min(tn, n)
    in_specs = [pl.BlockSpec((tm, kd), lambda j, i: (i, 0)),
                pl.BlockSpec((kd, tn), lambda j, i: (0, j))]
    args = [a, b]
    body = _mm_kernel
    if residual is not None:
        body = _mm_res_kernel
        in_specs.append(pl.BlockSpec((tm, tn), lambda j, i: (i, j)))
        args.append(residual)
    return pl.pallas_call(
        body,
        grid=(n // tn, m // tm),
        in_specs=in_specs,
        out_specs=pl.BlockSpec((tm, tn), lambda j, i: (i, j)),
        out_shape=jax.ShapeDtypeStruct((m, n), out_dtype),
        compiler_params=_params(2),
        name="matmul_res" if residual is not None else "matmul",
    )(*args)


def matmul_res_ktiled(a, b, residual, tm=1024, tn=1024, tk=2048):
    m, kd = a.shape
    n = b.shape[1]
    return pl.pallas_call(
        _mm_res_acc_kernel,
        grid=(m // tm, n // tn, kd // tk),
        in_specs=[pl.BlockSpec((tm, tk), lambda i, j, k: (i, k)),
                  pl.BlockSpec((tk, tn), lambda i, j, k: (k, j)),
                  pl.BlockSpec((tm, tn), lambda i, j, k: (i, j))],
        out_specs=pl.BlockSpec((tm, tn), lambda i, j, k: (i, j)),
        out_shape=jax.ShapeDtypeStruct((m, n), F32),
        compiler_params=_params(3),
        name="matmul_res_ktiled",
    )(a, b, residual)


def _gelu_exact(x):
    return 0.5 * x * (1.0 + lax.erf(x * (1.0 / math.sqrt(2.0))))


def _peer_act_kernel(h_ref, u_ref, g_ref, o_ref):
    tm, n_key0, n_key1 = g_ref.shape
    g = jnp.swapaxes(g_ref[...].reshape(tm // SUBLANES, SUBLANES, n_key0, n_key1), 1, 2)
    act = _gelu_exact(lax.dot_general(h_ref[...], u_ref[...], _NT, preferred_element_type=F32))
    for a in range(n_key0):
        cols = slice(a * n_key1, (a + 1) * n_key1)
        o_ref[:, cols] = (act[:, cols] * g[:, a].reshape(tm, n_key1)).astype(o_ref.dtype)


def peer_gated_act(hn, u, gates, tm=1024, tn=1024):
    t, d = hn.shape
    e = u.shape[0]
    tm = min(tm, t)
    n_key0 = tn // PEER_NKEYS
    return pl.pallas_call(
        _peer_act_kernel,
        grid=(e // tn, t // tm),
        in_specs=[pl.BlockSpec((tm, d), lambda j, i: (i, 0)),
                  pl.BlockSpec((tn, d), lambda j, i: (j, 0)),
                  pl.BlockSpec((tm, n_key0, PEER_NKEYS), lambda j, i: (i, j, 0))],
        out_specs=pl.BlockSpec((tm, tn), lambda j, i: (i, j)),
        out_shape=jax.ShapeDtypeStruct((t, e), BF16),
        compiler_params=_params(2),
        name="peer_gated_act",
    )(hn, u, gates)


def _rms_lanes(x, g):
    ms = jnp.mean(x * x, axis=-1, keepdims=True)
    return x * lax.rsqrt(ms + EPS) * g


def _rms_halves(x, g):
    lane = lax.broadcasted_iota(I32, x.shape, 1)
    lo = lane < DIFF_DH
    xx = x * x
    ms_lo = jnp.sum(jnp.where(lo, xx, 0.0), axis=-1, keepdims=True) * (1.0 / DIFF_DH)
    ms_hi = jnp.sum(jnp.where(lo, 0.0, xx), axis=-1, keepdims=True) * (1.0 / DIFF_DH)
    inv = jnp.where(lo, lax.rsqrt(ms_lo + EPS), lax.rsqrt(ms_hi + EPS))
    return x * inv * g


def _exp2_weights(s2):
    return jnp.exp2(s2 - jnp.max(s2, axis=-1, keepdims=True))


def _values_with_ones(v):
    ones_col = jnp.where(lax.broadcasted_iota(I32, v.shape, 1) == 0, 1.0, 0.0)
    return jnp.concatenate([v, ones_col], axis=-1).astype(BF16)


def _weighted_values(p, vx):
    ov = jnp.dot(p.astype(BF16), vx, preferred_element_type=F32)
    return ov[:, :HEAD_DIM], ov[:, HEAD_DIM:HEAD_DIM + 1]


def _alibi_tables_kernel(slope_ref, alibi_ref, dil_ref, mult_ref, *, s_len):
    shape = alibi_ref.shape
    d = lax.broadcasted_iota(I32, shape, 1) - lax.broadcasted_iota(I32, shape, 0) - (s_len - TAB_ROWS)
    ad = jnp.abs(d)
    alibi = (-LOG2E * slope_ref[pl.program_id(0)]) * ad.astype(F32)
    mult = jnp.zeros(shape, F32)
    for dil in DIL_DILATIONS:
        mult = mult + jnp.where(((d & (dil - 1)) == 0) & (ad <= DIL_HALF * dil), 1.0, 0.0)
    alibi_ref[...] = alibi
    dil_ref[...] = jnp.where(mult > 0.0, alibi, NEG_INF)
    mult_ref[...] = mult


def alibi_tables(s_len):
    slopes = jnp.asarray(2.0 ** (-8.0 * np.arange(1, N_HEADS + 1) / N_HEADS), F32)
    width = 2 * s_len - TAB_ROWS
    per_head = pl.BlockSpec((None, TAB_ROWS, width), lambda h: (h, 0, 0))
    return pl.pallas_call(
        functools.partial(_alibi_tables_kernel, s_len=s_len),
        grid=(N_HEADS,),
        in_specs=[pl.BlockSpec(memory_space=pltpu.SMEM)],
        out_specs=[per_head, per_head, pl.BlockSpec((TAB_ROWS, width), lambda h: (0, 0))],
        out_shape=[jax.ShapeDtypeStruct((N_HEADS, TAB_ROWS, width), F32),
                   jax.ShapeDtypeStruct((N_HEADS, TAB_ROWS, width), F32),
                   jax.ShapeDtypeStruct((TAB_ROWS, width), F32)],
        compiler_params=_params(1),
        name="alibi_tables",
    )(slopes)


def _table_tile(tab_ref, r0, k_lo, k_hi, s_len):
    row = r0 % TAB_ROWS
    lane0 = s_len - TAB_ROWS - (r0 - row) + k_lo
    return tab_ref[row:row + ATT_ROWS, lane0:lane0 + (k_hi - k_lo)]


def _head_specs(s, col0, width=HEAD_DIM):
    return [pl.BlockSpec((None, s, width), lambda h, bi, c=c: (bi, 0, col0 + c * N_HEADS + h)) for c in range(3)]


def _diff_attn_kernel(q_ref, k_ref, v_ref, gq_ref, gk_ref, go_ref, lam_ref, tab_ref, o_ref, kn_ref, vb_ref,
                      *, lambda_init):
    s_len = k_ref.shape[0]
    kn_ref[...] = _rms_halves(k_ref[...], gk_ref[...]).astype(BF16)
    vb_ref[...] = _values_with_ones(v_ref[...])
    lv = lam_ref[...]
    lam = (jnp.exp(jnp.sum(lv[0:1] * lv[1:2], axis=-1, keepdims=True))
           - jnp.exp(jnp.sum(lv[2:3] * lv[3:4], axis=-1, keepdims=True)) + lambda_init)
    lo = lax.broadcasted_iota(I32, (ATT_ROWS, HEAD_DIM), 1) < DIFF_DH
    for r0 in range(0, s_len, ATT_ROWS):
        qn = _rms_halves(q_ref[r0:r0 + ATT_ROWS, :], gq_ref[...]) * (DIFF_DH ** -0.5 * LOG2E)
        tab = _table_tile(tab_ref, r0, 0, s_len, s_len)
        p0 = _exp2_weights(lax.dot_general(jnp.where(lo, qn, 0.0).astype(BF16), kn_ref[...], _NT,
                                           preferred_element_type=F32) + tab)
        p1 = _exp2_weights(lax.dot_general(jnp.where(lo, 0.0, qn).astype(BF16), kn_ref[...], _NT,
                                           preferred_element_type=F32) + tab)
        o0, l0 = _weighted_values(p0, vb_ref[...])
        o1, l1 = _weighted_values(p1, vb_ref[...])
        o = o0 * (1.0 / l0) - o1 * (lam / l1)
        o_ref[r0:r0 + ATT_ROWS, :] = (_rms_lanes(o, go_ref[...]) * (1.0 - lambda_init)).astype(o_ref.dtype)


def diff_attention(p, col0, lam_vecs, g_qk, g_out, lambda_init, alibi):
    b, s, _ = p.shape
    lam_pad = jnp.pad(lam_vecs, ((0, 0), (0, HEAD_DIM - DIFF_DH)))
    gq = jnp.tile(g_qk[0], 2).reshape(1, HEAD_DIM)
    gk = jnp.tile(g_qk[1], 2).reshape(1, HEAD_DIM)
    vec = pl.BlockSpec((1, HEAD_DIM), lambda h, bi: (0, 0))
    return pl.pallas_call(
        functools.partial(_diff_attn_kernel, lambda_init=lambda_init),
        grid=(N_HEADS, b),
        in_specs=_head_specs(s, col0) + [
            vec, vec, vec,
            pl.BlockSpec((4, HEAD_DIM), lambda h, bi: (0, 0)),
            pl.BlockSpec((None,) + alibi.shape[1:], lambda h, bi: (h, 0, 0))],
        out_specs=pl.BlockSpec((None, s, HEAD_DIM), lambda h, bi: (bi, 0, h)),
        out_shape=jax.ShapeDtypeStruct((b, s, GROUP_WIDTH), BF16),
        scratch_shapes=[pltpu.VMEM((s, HEAD_DIM), BF16), pltpu.VMEM((s, 2 * HEAD_DIM), BF16)],
        compiler_params=_params(2),
        name="diff_attention",
    )(p, p, p, gq, gk, g_out.reshape(1, HEAD_DIM), lam_pad, alibi)


def _dilated_attn_kernel(q_ref, k_ref, v_ref, gq_ref, gk_ref, tab_ref, mult_ref, o_ref, kn_ref, vb_ref):
    s_len = k_ref.shape[0]
    kn_ref[...] = _rms_lanes(k_ref[...], gk_ref[...]).astype(BF16)
    vb_ref[...] = _values_with_ones(v_ref[...])
    for r0 in range(0, s_len, ATT_ROWS):
        k_lo, k_hi = max(0, r0 - DIL_REACH), min(s_len, r0 + ATT_ROWS + DIL_REACH)
        qn = (_rms_lanes(q_ref[r0:r0 + ATT_ROWS, :], gq_ref[...]) * (HEAD_DIM ** -0.5 * LOG2E)).astype(BF16)
        s2 = (lax.dot_general(qn, kn_ref[k_lo:k_hi, :], _NT, preferred_element_type=F32)
              + _table_tile(tab_ref, r0, k_lo, k_hi, s_len))
        o, l = _weighted_values(_exp2_weights(s2) * _table_tile(mult_ref, r0, k_lo, k_hi, s_len),
                                vb_ref[k_lo:k_hi, :])
        o_ref[r0:r0 + ATT_ROWS, :] = (o * (1.0 / l)).astype(o_ref.dtype)


def dilated_attention(p, col0, g_qk, dil_tab, mult_tab):
    b, s, _ = p.shape
    vec = pl.BlockSpec((1, HEAD_DIM), lambda h, bi: (0, 0))
    return pl.pallas_call(
        _dilated_attn_kernel,
        grid=(N_HEADS, b),
        in_specs=_head_specs(s, col0) + [
            vec, vec,
            pl.BlockSpec((None,) + dil_tab.shape[1:], lambda h, bi: (h, 0, 0)),
            pl.BlockSpec(mult_tab.shape, lambda h, bi: (0, 0))],
        out_specs=pl.BlockSpec((None, s, HEAD_DIM), lambda h, bi: (bi, 0, h)),
        out_shape=jax.ShapeDtypeStruct((b, s, GROUP_WIDTH), BF16),
        scratch_shapes=[pltpu.VMEM((s, HEAD_DIM), BF16), pltpu.VMEM((s, 2 * HEAD_DIM), BF16)],
        compiler_params=_params(2),
        name="dilated_attention",
    )(p, p, p, g_qk[0].reshape(1, HEAD_DIM), g_qk[1].reshape(1, HEAD_DIM), dil_tab, mult_tab)


def _na_window_start(q_row0, n_rows):
    return min(max(q_row0 - NA_ROWS // 2, 0), n_rows - NA_K_ROWS)


def _na_attn_kernel(q_ref, k_ref, v_ref, gq_ref, gk_ref, bias_ref, o_ref, kn_ref, vb_ref):
    n_rows = k_ref.shape[0] // GRID_W
    kn_ref[...] = _rms_lanes(k_ref[...], gk_ref[...]).astype(BF16)
    vb_ref[...] = _values_with_ones(v_ref[...])
    tq, tk = NA_Q_ROWS * GRID_W, NA_K_ROWS * GRID_W
    for qb in range(n_rows // NA_Q_ROWS):
        k0 = _na_window_start(qb * NA_Q_ROWS, n_rows) * GRID_W
        qn = (_rms_lanes(q_ref[qb * tq:(qb + 1) * tq, :], gq_ref[...]) * (HEAD_DIM ** -0.5 * LOG2E)).astype(BF16)
        p = _exp2_weights(lax.dot_general(qn, kn_ref[k0:k0 + tk, :], _NT, preferred_element_type=F32)
                          + bias_ref[qb])
        o, l = _weighted_values(p, vb_ref[k0:k0 + tk, :])
        o_ref[qb * tq:(qb + 1) * tq, :] = (o * (1.0 / l)).astype(o_ref.dtype)


def _na_bias_kernel(t_ref, o_ref, *, n_rows):
    lane = lax.broadcasted_iota(I32, (GRID_W, 2 * GRID_W), 1)
    for qb in range(n_rows // NA_Q_ROWS):
        q_row0 = qb * NA_Q_ROWS
        k_row0 = _na_window_start(q_row0, n_rows)
        for a in range(NA_Q_ROWS):
            rq = q_row0 + a
            rs = min(max(rq - NA_ROWS // 2, 0), n_rows - NA_ROWS)
            for wp in range(NA_K_ROWS // 2):
                rk = k_row0 + 2 * wp
                ok0, ok1 = rs <= rk < rs + NA_ROWS, rs <= rk + 1 < rs + NA_ROWS
                m = min(max(rk - rq + NA_ROWS, 0), 2 * NA_ROWS - 1)
                tile = t_ref[m] * LOG2E
                if not (ok0 and ok1):
                    keep = (lane < GRID_W) if ok0 else (lane >= GRID_W)
                    tile = jnp.where(keep, tile, NEG_INF) if (ok0 or ok1) else jnp.full_like(tile, NEG_INF)
                o_ref[qb, a * GRID_W:(a + 1) * GRID_W, wp * 2 * GRID_W:(wp + 1) * 2 * GRID_W] = tile


def _na_bias(rpb, n_rows):
    n_rel_r = 2 * NA_ROWS - 1
    pad = GRID_W - NA_COLS
    rp = jnp.pad(rpb, ((0, 0), (0, 0), (pad, pad)))
    toe = jnp.stack([rp[:, :, NA_COLS - 1 + pad - cq:NA_COLS - 1 + pad - cq + GRID_W] for cq in range(GRID_W)],
                    axis=2)
    cq = np.arange(GRID_W)
    cs = np.clip(cq - NA_COLS // 2, 0, GRID_W - NA_COLS)
    col_ok = (cq[None, :] >= cs[:, None]) & (cq[None, :] < cs[:, None] + NA_COLS)
    toe = jnp.where(col_ok[None, None], toe, NEG_INF)
    ext = jnp.pad(toe, ((0, 0), (1, 1), (0, 0), (0, 0)), constant_values=NEG_INF)
    pairs = jnp.concatenate([ext[:, :n_rel_r + 1], ext[:, 1:]], axis=-1)
    tq, tk = NA_Q_ROWS * GRID_W, NA_K_ROWS * GRID_W
    nqb = n_rows // NA_Q_ROWS
    return pl.pallas_call(
        functools.partial(_na_bias_kernel, n_rows=n_rows),
        grid=(N_HEADS,),
        in_specs=[pl.BlockSpec((None, n_rel_r + 1, GRID_W, 2 * GRID_W), lambda h: (h, 0, 0, 0))],
        out_specs=pl.BlockSpec((None, nqb, tq, tk), lambda h: (h, 0, 0, 0)),
        out_shape=jax.ShapeDtypeStruct((N_HEADS, nqb, tq, tk), F32),
        compiler_params=_params(1),
        name="na_bias",
    )(pairs)


def neighbourhood_attention(p, col0, g_qk, rpb):
    b, s, _ = p.shape
    bias = _na_bias(rpb, s // GRID_W)
    vec = pl.BlockSpec((1, HEAD_DIM), lambda h, bi: (0, 0))
    return pl.pallas_call(
        _na_attn_kernel,
        grid=(N_HEADS, b),
        in_specs=_head_specs(s, col0) + [
            vec, vec,
            pl.BlockSpec((None,) + bias.shape[1:], lambda h, bi: (h, 0, 0, 0))],
        out_specs=pl.BlockSpec((None, s, HEAD_DIM), lambda h, bi: (bi, 0, h)),
        out_shape=jax.ShapeDtypeStruct((b, s, GROUP_WIDTH), BF16),
        scratch_shapes=[pltpu.VMEM((s, HEAD_DIM), BF16), pltpu.VMEM((s, 2 * HEAD_DIM), BF16)],
        compiler_params=_params(2),
        name="neighbourhood_attention",
    )(p, p, p, g_qk[0].reshape(1, HEAD_DIM), g_qk[1].reshape(1, HEAD_DIM), bias)


def _mla_norm_rope(x, g, cos, sin):
    ms = jnp.sum(x * x, axis=-1, keepdims=True) * (1.0 / MLA_QK)
    xn = x * lax.rsqrt(ms + EPS) * g
    xr = xn[:, HEAD_DIM:]
    partner = pltpu.roll(xr, QK_ROPE // 2, 1) + pltpu.roll(xr, HEAD_DIM - QK_ROPE // 2, 1)
    return jnp.concatenate([xn[:, :HEAD_DIM], xr * cos + partner * sin], axis=-1)


def _mla_attn_kernel(q_ref, kn_in_ref, v_ref, kr_ref, cos_ref, sin_ref, gq_ref, gk_ref, o_ref, kf_ref, vb_ref):
    s_len = v_ref.shape[0]
    k = jnp.concatenate([kn_in_ref[...], kr_ref[...]], axis=-1)
    kf_ref[...] = _mla_norm_rope(k, gk_ref[...], cos_ref[...], sin_ref[...]).astype(BF16)
    vb_ref[...] = _values_with_ones(v_ref[...])
    def scores(r0):
        rows = slice(r0, r0 + ATT_ROWS)
        qf = (_mla_norm_rope(q_ref[rows, :], gq_ref[...], cos_ref[rows, :], sin_ref[rows, :])
              * (MLA_QK ** -0.5 * LOG2E)).astype(BF16)
        return lax.dot_general(qf, kf_ref[...], _NT, preferred_element_type=F32)

    s2_next = scores(0)
    for r0 in range(0, s_len, ATT_ROWS):
        s2 = s2_next
        if r0 + ATT_ROWS < s_len:
            s2_next = scores(r0 + ATT_ROWS)
        o, l = _weighted_values(_exp2_weights(s2), vb_ref[...])
        o_ref[r0:r0 + ATT_ROWS, :] = (o * (1.0 / l)).astype(o_ref.dtype)


def mla_attention(q_up, kv_up, k_rope, g_qk, cos_t, sin_t):
    b, s, _ = q_up.shape
    pad = MLA_QK_PAD - MLA_QK
    gq = jnp.pad(g_qk[0], (0, pad)).reshape(1, MLA_QK_PAD)
    gk = jnp.pad(g_qk[1], (0, pad)).reshape(1, MLA_QK_PAD)
    vec = pl.BlockSpec((1, MLA_QK_PAD), lambda bi, h: (0, 0))
    rope = pl.BlockSpec((s, HEAD_DIM), lambda bi, h: (0, 0))
    return pl.pallas_call(
        _mla_attn_kernel,
        grid=(b, N_HEADS),
        in_specs=[pl.BlockSpec((None, s, MLA_QK_PAD), lambda bi, h: (bi, 0, h)),
                  pl.BlockSpec((None, s, HEAD_DIM), lambda bi, h: (bi, 0, 2 * h)),
                  pl.BlockSpec((None, s, HEAD_DIM), lambda bi, h: (bi, 0, 2 * h + 1)),
                  pl.BlockSpec((None, s, HEAD_DIM), lambda bi, h: (bi, 0, 0)),
                  rope, rope, vec, vec],
        out_specs=pl.BlockSpec((None, s, HEAD_DIM), lambda bi, h: (bi, 0, h)),
        out_shape=jax.ShapeDtypeStruct((b, s, GROUP_WIDTH), BF16),
        scratch_shapes=[pltpu.VMEM((s, MLA_QK_PAD), BF16), pltpu.VMEM((s, 2 * HEAD_DIM), BF16)],
        compiler_params=_params(2),
        name="mla_attention",
    )(q_up, kv_up, kv_up, k_rope, cos_t, sin_t, gq, gk)


def _top_k_lead(s, order, payload=None):
    vals, picks = [], []
    for _ in range(PEER_TOPK):
        m = jnp.max(s, axis=0, keepdims=True)
        first = jnp.min(jnp.where(s == m, order, jnp.iinfo(jnp.int32).max), axis=0, keepdims=True)
        sel = order == first
        vals.append(m)
        picks.append(first if payload is None else jnp.max(jnp.where(sel, payload, 0), axis=0, keepdims=True))
        s = jnp.where(sel, -jnp.inf, s)
    return jnp.concatenate(vals, axis=0), jnp.concatenate(picks, axis=0)


_PAIRS = tuple((i, j) for i in range(PEER_TOPK) for j in range(PEER_TOPK) if (i + 1) * (j + 1) <= PEER_TOPK)


def _peer_route_kernel(q_ref, keys_ref, expert_ref, gate_ref):
    groups = q_ref.shape[0] // HEAD_DIM
    tile = (1, groups, HEAD_DIM)
    key_id = lax.broadcasted_iota(I32, (PEER_NKEYS,) + tile[1:], 0)
    cand_flat = jnp.concatenate([jnp.full(tile, i * PEER_TOPK + j, I32) for i, j in _PAIRS], axis=0)
    tops = []
    for hc in range(keys_ref.shape[0]):
        cols = slice(hc * HEAD_DIM, (hc + 1) * HEAD_DIM)
        sc = jnp.stack([lax.dot_general(keys_ref[hc], q_ref[g * HEAD_DIM:(g + 1) * HEAD_DIM, cols].astype(BF16),
                                        _NT, preferred_element_type=F32) for g in range(groups)], axis=0)
        tops.append(_top_k_lead(jnp.swapaxes(sc, 0, 1), key_id))
    for hh in range(keys_ref.shape[0] // 2):
        (s0, i0), (s1, i1) = tops[2 * hh], tops[2 * hh + 1]
        cand_s = jnp.concatenate([s0[i:i + 1] + s1[j:j + 1] for i, j in _PAIRS], axis=0)
        cand_e = jnp.concatenate([i0[i:i + 1] * PEER_NKEYS + i1[j:j + 1] for i, j in _PAIRS], axis=0)
        best_s, best_e = _top_k_lead(cand_s, cand_flat, cand_e)
        e = jnp.exp(best_s - jnp.max(best_s, axis=0, keepdims=True))
        gate = jnp.swapaxes(e * (1.0 / jnp.sum(e, axis=0, keepdims=True)), 0, 1)
        expert = jnp.swapaxes(best_e, 0, 1)
        rows = slice(hh * PEER_TOPK, (hh + 1) * PEER_TOPK)
        for g in range(groups):
            gate_ref[rows, g * HEAD_DIM:(g + 1) * HEAD_DIM] = gate[g]
            expert_ref[rows, g * HEAD_DIM:(g + 1) * HEAD_DIM] = expert[g]


def peer_route(q, sub_keys, tt=1024, heads_per_step=2):
    t = q.shape[0]
    n_sel = PEER_HEADS * PEER_TOPK
    hs = heads_per_step
    return pl.pallas_call(
        _peer_route_kernel,
        grid=(t // tt, PEER_HEADS // hs),
        in_specs=[pl.BlockSpec((tt, 2 * hs * HEAD_DIM), lambda i, h: (i, h)),
                  pl.BlockSpec((2 * hs, PEER_NKEYS, HEAD_DIM), lambda i, h: (h, 0, 0))],
        out_specs=[pl.BlockSpec((hs * PEER_TOPK, tt), lambda i, h: (h, i)),
                   pl.BlockSpec((hs * PEER_TOPK, tt), lambda i, h: (h, i))],
        out_shape=[jax.ShapeDtypeStruct((n_sel, t), I32), jax.ShapeDtypeStruct((n_sel, t), F32)],
        compiler_params=_params(2),
        name="peer_route",
    )(q, sub_keys)


def _peer_gate_matrix_kernel(expert_ref, gate_ref, o_ref, e_s, g_s):
    e_s[...] = expert_ref[...].T
    g_s[...] = gate_ref[...].T
    n_sel = e_s.shape[1]
    row = lax.broadcasted_iota(I32, (PEER_NKEYS, n_sel), 0)

    def body(t, carry):
        e_row = e_s[pl.ds(t, 1), :]
        g_row = g_s[pl.ds(t, 1), :]
        hi = jnp.where(row == (e_row >> 7), g_row, 0.0).astype(BF16)
        lo = jnp.where(row == (e_row & (PEER_NKEYS - 1)), 1.0, 0.0).astype(BF16)
        o_ref[t] = lax.dot_general(hi, lo, _NT, preferred_element_type=F32)
        return carry

    lax.fori_loop(0, e_s.shape[0], body, 0, unroll=16)


def peer_gate_matrix(expert_t, gate_t, tg=128):
    n_sel, t = expert_t.shape
    return pl.pallas_call(
        _peer_gate_matrix_kernel,
        grid=(t // tg,),
        in_specs=[pl.BlockSpec((n_sel, tg), lambda i: (0, i)), pl.BlockSpec((n_sel, tg), lambda i: (0, i))],
        out_specs=pl.BlockSpec((tg, PEER_NKEYS, PEER_NKEYS), lambda i: (i, 0, 0)),
        out_shape=jax.ShapeDtypeStruct((t, PEER_NKEYS, PEER_NKEYS), F32),
        scratch_shapes=[pltpu.VMEM((tg, n_sel), I32), pltpu.VMEM((tg, n_sel), F32)],
        compiler_params=_params(1),
        name="peer_gate_matrix",
    )(expert_t, gate_t)


def _rope_tables(s):
    inv_freq = 1.0 / (ROPE_THETA ** (np.arange(0, QK_ROPE, 2, dtype=np.float32) / QK_ROPE))
    ang = jnp.arange(s, dtype=F32)[:, None] * jnp.asarray(inv_freq, F32)[None, :]
    cos, sin = jnp.cos(ang), jnp.sin(ang)
    zeros = jnp.zeros((s, HEAD_DIM - QK_ROPE), F32)
    return (jnp.concatenate([cos, cos, zeros], axis=-1), jnp.concatenate([-sin, sin, zeros], axis=-1))


def _layer(x2, b, s, l, tables, g_mix, w_in, w_out, diff_lambda, diff_g_qk, diff_g_out, dil_g_qk, na_g_qk,
           na_rpb, mla_g_cq, mla_g_ckv, mla_w_uq, mla_w_ukv, mla_g_qk, g_ffn, peer_w_q, peer_sub_keys, peer_u,
           peer_v):
    t = b * s
    cos_t, sin_t, alibi, dil_tab, mult_tab = tables
    lambda_init = 0.8 - 0.6 * math.exp(-0.3 * l)
    n_qkv = 3 * QKV_COLS

    h = rms_rows(x2, g_mix)
    p = matmul(h, w_in[:, :n_qkv].astype(BF16), tm=1024).reshape(b, s, n_qkv)
    w_lat = jnp.pad(w_in[:, n_qkv:], ((0, 0), (0, HEAD_DIM - QK_ROPE))).astype(BF16)
    p_lat = matmul(h, w_lat, tn=MLA_LATENT_PAD)
    c_q, c_kv, k_rope = mla_latent_norm(p_lat, mla_g_cq, mla_g_ckv)

    blocks = QKV_COLS // HEAD_DIM
    o_a = diff_attention(p, 0, diff_lambda, diff_g_qk, diff_g_out, lambda_init, alibi)
    o_b = dilated_attention(p, blocks, dil_g_qk, dil_tab, mult_tab)
    o_c = neighbourhood_attention(p, 2 * blocks, na_g_qk, na_rpb)

    w_uq = jnp.pad(mla_w_uq.reshape(Q_LORA, N_HEADS, MLA_QK), ((0, 0), (0, 0), (0, MLA_QK_PAD - MLA_QK)))
    q_up = matmul(c_q, w_uq.reshape(Q_LORA, N_HEADS * MLA_QK_PAD).astype(BF16), tm=1024)
    kv_up = matmul(c_kv, mla_w_ukv.astype(BF16), tm=1024)
    o_d = mla_attention(q_up.reshape(b, s, -1), kv_up.reshape(b, s, -1), k_rope.reshape(b, s, HEAD_DIM),
                        mla_g_qk, cos_t, sin_t)

    mixed = jnp.concatenate([o_a, o_b, o_c, o_d], axis=-1).reshape(t, 4 * GROUP_WIDTH)
    x2 = matmul(mixed, w_out.astype(BF16), residual=x2)

    hf = rms_rows(x2, g_ffn)
    q = matmul(hf, peer_w_q.astype(BF16), tm=1024)
    keys = peer_sub_keys.reshape(2 * PEER_HEADS, PEER_NKEYS, HEAD_DIM).astype(BF16)
    expert_t, gate_t = peer_route(q, keys)
    gates = peer_gate_matrix(expert_t, gate_t)
    act = peer_gated_act(hf, peer_u.astype(BF16), gates)
    return matmul_res_ktiled(act, peer_v.astype(BF16), x2)


def kernel(x, g_mix, w_in, w_out, diff_lambda, diff_g_qk, diff_g_out, dil_g_qk, na_g_qk, na_rpb, mla_g_cq,
           mla_g_ckv, mla_w_uq, mla_w_ukv, mla_g_qk, g_ffn, peer_w_q, peer_sub_keys, peer_u, peer_v):
    b, s, d = x.shape
    depth = g_mix.shape[0]
    tables = _rope_tables(s) + tuple(alibi_tables(s))
    x2 = x.reshape(b * s, d)
    per_layer = (g_mix, w_in, w_out, diff_lambda, diff_g_qk, diff_g_out, dil_g_qk, na_g_qk, na_rpb, mla_g_cq,
                 mla_g_ckv, mla_w_uq, mla_w_ukv, mla_g_qk, g_ffn, peer_w_q, peer_sub_keys, peer_u, peer_v)
    for l in range(depth):
        x2 = _layer(x2, b, s, l, tables, *(w[l] for w in per_layer))
    return x2.reshape(b, s, d)
```

```python
import functools
import math

import jax
import jax.numpy as jnp
import numpy as np
from jax import lax
from jax.experimental import pallas as pl
from jax.experimental.pallas import tpu as pltpu

F32 = jnp.float32
BF16 = jnp.bfloat16
I32 = jnp.int32

SUBLANES = 8
HEAD_DIM = 128
N_HEADS = 8
GROUP_WIDTH = N_HEADS * HEAD_DIM
QKV_COLS = 3 * GROUP_WIDTH
EPS = 1e-6
NEG_INF = -1e30
LOG2E = math.log2(math.e)
DIFF_DH = HEAD_DIM // 2
DIL_HALF = 64
DIL_DILATIONS = (1, 4, 16)
DIL_REACH = DIL_HALF * max(DIL_DILATIONS)
GRID_W = 64
NA_ROWS = 8
NA_COLS = 16
NA_Q_ROWS = 4
NA_K_ROWS = NA_Q_ROWS + NA_ROWS
Q_LORA = 768
KV_LORA = 512
QK_NOPE = 128
QK_ROPE = 64
MLA_QK = QK_NOPE + QK_ROPE
MLA_QK_PAD = 2 * HEAD_DIM
ROPE_THETA = 10000.0
PEER_HEADS = 8
PEER_NKEYS = 128
PEER_TOPK = 16
MLA_LATENT_PAD = Q_LORA + KV_LORA + HEAD_DIM
ATT_ROWS = 256
TAB_ROWS = 512

VMEM_LIMIT = 54 * 1024 * 1024

_NT = (((1,), (1,)), ((), ()))


def _params(grid_rank):
    return pltpu.CompilerParams(dimension_semantics=("arbitrary",) * grid_rank, vmem_limit_bytes=VMEM_LIMIT)


def _rms_rows_kernel(x_ref, g_ref, o_ref):
    x = x_ref[...]
    ms = jnp.mean(x * x, axis=-1, keepdims=True)
    o_ref[...] = (x * lax.rsqrt(ms + EPS) * g_ref[...]).astype(o_ref.dtype)


def rms_rows(x, g, tm=256):
    t, c = x.shape
    return pl.pallas_call(
        _rms_rows_kernel,
        grid=(t // tm,),
        in_specs=[pl.BlockSpec((tm, c), lambda i: (i, 0)), pl.BlockSpec((1, c), lambda i: (0, 0))],
        out_specs=pl.BlockSpec((tm, c), lambda i: (i, 0)),
        out_shape=jax.ShapeDtypeStruct((t, c), BF16),
        compiler_params=_params(1),
        name="rms_rows",
    )(x, g.reshape(1, c))


def _mla_latent_kernel(p_ref, gq_ref, gkv_ref, cq_ref, ckv_ref, kr_ref):
    cq = p_ref[:, :Q_LORA]
    ckv = p_ref[:, Q_LORA:Q_LORA + KV_LORA]
    cq_ref[...] = (cq * lax.rsqrt(jnp.mean(cq * cq, axis=-1, keepdims=True) + EPS) * gq_ref[...]).astype(BF16)
    ckv_ref[...] = (ckv * lax.rsqrt(jnp.mean(ckv * ckv, axis=-1, keepdims=True) + EPS) * gkv_ref[...]).astype(BF16)
    kr_ref[...] = p_ref[:, Q_LORA + KV_LORA:]


def mla_latent_norm(p_lat, g_cq, g_ckv, tm=512):
    t = p_lat.shape[0]
    return pl.pallas_call(
        _mla_latent_kernel,
        grid=(t // tm,),
        in_specs=[pl.BlockSpec((tm, MLA_LATENT_PAD), lambda i: (i, 0)),
                  pl.BlockSpec((1, Q_LORA), lambda i: (0, 0)),
                  pl.BlockSpec((1, KV_LORA), lambda i: (0, 0))],
        out_specs=[pl.BlockSpec((tm, Q_LORA), lambda i: (i, 0)),
                   pl.BlockSpec((tm, KV_LORA), lambda i: (i, 0)),
                   pl.BlockSpec((tm, HEAD_DIM), lambda i: (i, 0))],
        out_shape=[jax.ShapeDtypeStruct((t, Q_LORA), BF16),
                   jax.ShapeDtypeStruct((t, KV_LORA), BF16),
                   jax.ShapeDtypeStruct((t, HEAD_DIM), F32)],
        compiler_params=_params(1),
        name="mla_latent_norm",
    )(p_lat, g_cq.reshape(1, Q_LORA), g_ckv.reshape(1, KV_LORA))


def _mm_kernel(a_ref, b_ref, o_ref):
    o_ref[...] = jnp.dot(a_ref[...], b_ref[...], preferred_element_type=F32).astype(o_ref.dtype)


def _mm_res_kernel(a_ref, b_ref, r_ref, o_ref):
    o_ref[...] = r_ref[...] + jnp.dot(a_ref[...], b_ref[...], preferred_element_type=F32)


def _mm_res_acc_kernel(a_ref, b_ref, r_ref, o_ref):
    @pl.when(pl.program_id(2) == 0)
    def _():
        o_ref[...] = r_ref[...]

    o_ref[...] += jnp.dot(a_ref[...], b_ref[...], preferred_element_type=F32)


def matmul(a, b, *, out_dtype=F32, residual=None, tm=512, tn=1024):
    m, kd = a.shape
    n = b.shape[1]
    tm, tn = min(tm, m), min(tn, n)
    in_specs = [pl.BlockSpec((tm, kd), lambda j, i: (i, 0)),
                pl.BlockSpec((kd, tn), lambda j, i: (0, j))]
    args = [a, b]
    body = _mm_kernel
    if residual is not None:
        body = _mm_res_kernel
        in_specs.append(pl.BlockSpec((tm, tn), lambda j, i: (i, j)))
        args.append(residual)
    return pl.pallas_call(
        body,
        grid=(n // tn, m // tm),
        in_specs=in_specs,
        out_specs=pl.BlockSpec((tm, tn), lambda j, i: (i, j)),
        out_shape=jax.ShapeDtypeStruct((m, n), out_dtype),
        compiler_params=_params(2),
        name="matmul_res" if residual is not None else "matmul",
    )(*args)


def matmul_res_ktiled(a, b, residual, tm=1024, tn=1024, tk=2048):
    m, kd = a.shape
    n = b.shape[1]
    return pl.pallas_call(
        _mm_res_acc_kernel,
        grid=(m // tm, n // tn, kd // tk),
        in_specs=[pl.BlockSpec((tm, tk), lambda i, j, k: (i, k)),
                  pl.BlockSpec((tk, tn), lambda i, j, k: (k, j)),
                  pl.BlockSpec((tm, tn), lambda i, j, k: (i, j))],
        out_specs=pl.BlockSpec((tm, tn), lambda i, j, k: (i, j)),
        out_shape=jax.ShapeDtypeStruct((m, n), F32),
        compiler_params=_params(3),
        name="matmul_res_ktiled",
    )(a, b, residual)


def _gelu_exact(x):
    return 0.5 * x * (1.0 + lax.erf(x * (1.0 / math.sqrt(2.0))))


def _peer_act_kernel(h_ref, u_ref, g_ref, o_ref):
    tm, n_key0, n_key1 = g_ref.shape
    g = jnp.swapaxes(g_ref[...].reshape(tm // SUBLANES, SUBLANES, n_key0, n_key1), 1, 2)
    act = _gelu_exact(lax.dot_general(h_ref[...], u_ref[...], _NT, preferred_element_type=F32))
    for a in range(n_key0):
        cols = slice(a * n_key1, (a + 1) * n_key1)
        o_ref[:, cols] = (act[:, cols] * g[:, a].reshape(tm, n_key1)).astype(o_ref.dtype)


def peer_gated_act(hn, u, gates, tm=1024, tn=1024):
    t, d = hn.shape
    e = u.shape[0]
    tm = min(tm, t)
    n_key0 = tn // PEER_NKEYS
    return pl.pallas_call(
        _peer_act_kernel,
        grid=(e // tn, t // tm),
        in_specs=[pl.BlockSpec((tm, d), lambda j, i: (i, 0)),
                  pl.BlockSpec((tn, d), lambda j, i: (j, 0)),
                  pl.BlockSpec((tm, n_key0, PEER_NKEYS), lambda j, i: (i, j, 0))],
        out_specs=pl.BlockSpec((tm, tn), lambda j, i: (i, j)),
        out_shape=jax.ShapeDtypeStruct((t, e), BF16),
        compiler_params=_params(2),
        name="peer_gated_act",
    )(hn, u, gates)


def _rms_lanes(x, g):
    ms = jnp.mean(x * x, axis=-1, keepdims=True)
    return x * lax.rsqrt(ms + EPS) * g


def _rms_halves(x, g):
    lane = lax.broadcasted_iota(I32, x.shape, 1)
    lo = lane < DIFF_DH
    xx = x * x
    ms_lo = jnp.sum(jnp.where(lo, xx, 0.0), axis=-1, keepdims=True) * (1.0 / DIFF_DH)
    ms_hi = jnp.sum(jnp.where(lo, 0.0, xx), axis=-1, keepdims=True) * (1.0 / DIFF_DH)
    inv = jnp.where(lo, lax.rsqrt(ms_lo + EPS), lax.rsqrt(ms_hi + EPS))
    return x * inv * g


def _exp2_weights(s2):
    return jnp.exp2(s2 - jnp.max(s2, axis=-1, keepdims=True))


def _values_with_ones(v):
    ones_col = jnp.where(lax.broadcasted_iota(I32, v.shape, 1) == 0, 1.0, 0.0)
    return jnp.concatenate([v, ones_col], axis=-1).astype(BF16)


def _weighted_values(p, vx):
    ov = jnp.dot(p.astype(BF16), vx, preferred_element_type=F32)
    return ov[:, :HEAD_DIM], ov[:, HEAD_DIM:HEAD_DIM + 1]


def _alibi_tables_kernel(slope_ref, alibi_ref, dil_ref, mult_ref, *, s_len):
    shape = alibi_ref.shape
    d = lax.broadcasted_iota(I32, shape, 1) - lax.broadcasted_iota(I32, shape, 0) - (s_len - TAB_ROWS)
    ad = jnp.abs(d)
    alibi = (-LOG2E * slope_ref[pl.program_id(0)]) * ad.astype(F32)
    mult = jnp.zeros(shape, F32)
    for dil in DIL_DILATIONS:
        mult = mult + jnp.where(((d & (dil - 1)) == 0) & (ad <= DIL_HALF * dil), 1.0, 0.0)
    alibi_ref[...] = alibi
    dil_ref[...] = jnp.where(mult > 0.0, alibi, NEG_INF)
    mult_ref[...] = mult


def alibi_tables(s_len):
    slopes = jnp.asarray(2.0 ** (-8.0 * np.arange(1, N_HEADS + 1) / N_HEADS), F32)
    width = 2 * s_len - TAB_ROWS
    per_head = pl.BlockSpec((None, TAB_ROWS, width), lambda h: (h, 0, 0))
    return pl.pallas_call(
        functools.partial(_alibi_tables_kernel, s_len=s_len),
        grid=(N_HEADS,),
        in_specs=[pl.BlockSpec(memory_space=pltpu.SMEM)],
        out_specs=[per_head, per_head, pl.BlockSpec((TAB_ROWS, width), lambda h: (0, 0))],
        out_shape=[jax.ShapeDtypeStruct((N_HEADS, TAB_ROWS, width), F32),
                   jax.ShapeDtypeStruct((N_HEADS, TAB_ROWS, width), F32),
                   jax.ShapeDtypeStruct((TAB_ROWS, width), F32)],
        compiler_params=_params(1),
        name="alibi_tables",
    )(slopes)


def _table_tile(tab_ref, r0, k_lo, k_hi, s_len):
    row = r0 % TAB_ROWS
    lane0 = s_len - TAB_ROWS - (r0 - row) + k_lo
    return tab_ref[row:row + ATT_ROWS, lane0:lane0 + (k_hi - k_lo)]


def _head_specs(s, col0, width=HEAD_DIM):
    return [pl.BlockSpec((None, s, width), lambda h, bi, c=c: (bi, 0, col0 + c * N_HEADS + h)) for c in range(3)]


def _diff_attn_kernel(q_ref, k_ref, v_ref, gq_ref, gk_ref, go_ref, lam_ref, tab_ref, o_ref, kn_ref, vb_ref,
                      *, lambda_init):
    s_len = k_ref.shape[0]
    kn_ref[...] = _rms_halves(k_ref[...], gk_ref[...]).astype(BF16)
    vb_ref[...] = _values_with_ones(v_ref[...])
    lv = lam_ref[...]
    lam = (jnp.exp(jnp.sum(lv[0:1] * lv[1:2], axis=-1, keepdims=True))
           - jnp.exp(jnp.sum(lv[2:3] * lv[3:4], axis=-1, keepdims=True)) + lambda_init)
    lo = lax.broadcasted_iota(I32, (ATT_ROWS, HEAD_DIM), 1) < DIFF_DH
    def scores(r0):
        qn = _rms_halves(q_ref[r0:r0 + ATT_ROWS, :], gq_ref[...]) * (DIFF_DH ** -0.5 * LOG2E)
        tab = _table_tile(tab_ref, r0, 0, s_len, s_len)
        return tuple(lax.dot_general(qc.astype(BF16), kn_ref[...], _NT, preferred_element_type=F32) + tab
                     for qc in (jnp.where(lo, qn, 0.0), jnp.where(lo, 0.0, qn)))

    s2_next = scores(0)
    for r0 in range(0, s_len, ATT_ROWS):
        s2 = s2_next
        if r0 + ATT_ROWS < s_len:
            s2_next = scores(r0 + ATT_ROWS)
        o0, l0 = _weighted_values(_exp2_weights(s2[0]), vb_ref[...])
        o1, l1 = _weighted_values(_exp2_weights(s2[1]), vb_ref[...])
        o = o0 * (1.0 / l0) - o1 * (lam / l1)
        o_ref[r0:r0 + ATT_ROWS, :] = (_rms_lanes(o, go_ref[...]) * (1.0 - lambda_init)).astype(o_ref.dtype)


def diff_attention(p, col0, lam_vecs, g_qk, g_out, lambda_init, alibi):
    b, s, _ = p.shape
    lam_pad = jnp.pad(lam_vecs, ((0, 0), (0, HEAD_DIM - DIFF_DH)))
    gq = jnp.tile(g_qk[0], 2).reshape(1, HEAD_DIM)
    gk = jnp.tile(g_qk[1], 2).reshape(1, HEAD_DIM)
    vec = pl.BlockSpec((1, HEAD_DIM), lambda h, bi: (0, 0))
    return pl.pallas_call(
        functools.partial(_diff_attn_kernel, lambda_init=lambda_init),
        grid=(N_HEADS, b),
        in_specs=_head_specs(s, col0) + [
            vec, vec, vec,
            pl.BlockSpec((4, HEAD_DIM), lambda h, bi: (0, 0)),
            pl.BlockSpec((None,) + alibi.shape[1:], lambda h, bi: (h, 0, 0))],
        out_specs=pl.BlockSpec((None, s, HEAD_DIM), lambda h, bi: (bi, 0, h)),
        out_shape=jax.ShapeDtypeStruct((b, s, GROUP_WIDTH), BF16),
        scratch_shapes=[pltpu.VMEM((s, HEAD_DIM), BF16), pltpu.VMEM((s, 2 * HEAD_DIM), BF16)],
        compiler_params=_params(2),
        name="diff_attention",
    )(p, p, p, gq, gk, g_out.reshape(1, HEAD_DIM), lam_pad, alibi)


def _dilated_attn_kernel(q_ref, k_ref, v_ref, gq_ref, gk_ref, tab_ref, mult_ref, o_ref, kn_ref, vb_ref):
    s_len = k_ref.shape[0]
    kn_ref[...] = _rms_lanes(k_ref[...], gk_ref[...]).astype(BF16)
    vb_ref[...] = _values_with_ones(v_ref[...])
    def key_window(r0):
        return max(0, r0 - DIL_REACH), min(s_len, r0 + ATT_ROWS + DIL_REACH)

    def scores(r0):
        k_lo, k_hi = key_window(r0)
        qn = (_rms_lanes(q_ref[r0:r0 + ATT_ROWS, :], gq_ref[...]) * (HEAD_DIM ** -0.5 * LOG2E)).astype(BF16)
        return (lax.dot_general(qn, kn_ref[k_lo:k_hi, :], _NT, preferred_element_type=F32)
                + _table_tile(tab_ref, r0, k_lo, k_hi, s_len))

    s2_next = scores(0)
    for r0 in range(0, s_len, ATT_ROWS):
        k_lo, k_hi = key_window(r0)
        s2 = s2_next
        if r0 + ATT_ROWS < s_len:
            s2_next = scores(r0 + ATT_ROWS)
        o, l = _weighted_values(_exp2_weights(s2) * _table_tile(mult_ref, r0, k_lo, k_hi, s_len),
                                vb_ref[k_lo:k_hi, :])
        o_ref[r0:r0 + ATT_ROWS, :] = (o * (1.0 / l)).astype(o_ref.dtype)


def dilated_attention(p, col0, g_qk, dil_tab, mult_tab):
    b, s, _ = p.shape
    vec = pl.BlockSpec((1, HEAD_DIM), lambda h, bi: (0, 0))
    return pl.pallas_call(
        _dilated_attn_kernel,
        grid=(N_HEADS, b),
        in_specs=_head_specs(s, col0) + [
            vec, vec,
            pl.BlockSpec((None,) + dil_tab.shape[1:], lambda h, bi: (h, 0, 0)),
            pl.BlockSpec(mult_tab.shape, lambda h, bi: (0, 0))],
        out_specs=pl.BlockSpec((None, s, HEAD_DIM), lambda h, bi: (bi, 0, h)),
        out_shape=jax.ShapeDtypeStruct((b, s, GROUP_WIDTH), BF16),
        scratch_shapes=[pltpu.VMEM((s, HEAD_DIM), BF16), pltpu.VMEM((s, 2 * HEAD_DIM), BF16)],
        compiler_params=_params(2),
        name="dilated_attention",
    )(p, p, p, g_qk[0].reshape(1, HEAD_DIM), g_qk[1].reshape(1, HEAD_DIM), dil_tab, mult_tab)


def _na_window_start(q_row0, n_rows):
    return min(max(q_row0 - NA_ROWS // 2, 0), n_rows - NA_K_ROWS)


def _na_attn_kernel(q_ref, k_ref, v_ref, gq_ref, gk_ref, bias_ref, o_ref, kn_ref, vb_ref):
    n_rows = k_ref.shape[0] // GRID_W
    kn_ref[...] = _rms_lanes(k_ref[...], gk_ref[...]).astype(BF16)
    vb_ref[...] = _values_with_ones(v_ref[...])
    tq, tk = NA_Q_ROWS * GRID_W, NA_K_ROWS * GRID_W
    n_blocks = n_rows // NA_Q_ROWS

    def scores(qb):
        k0 = _na_window_start(qb * NA_Q_ROWS, n_rows) * GRID_W
        qn = (_rms_lanes(q_ref[qb * tq:(qb + 1) * tq, :], gq_ref[...]) * (HEAD_DIM ** -0.5 * LOG2E)).astype(BF16)
        return lax.dot_general(qn, kn_ref[k0:k0 + tk, :], _NT, preferred_element_type=F32) + bias_ref[qb]

    s2_next = scores(0)
    for qb in range(n_blocks):
        k0 = _na_window_start(qb * NA_Q_ROWS, n_rows) * GRID_W
        s2 = s2_next
        if qb + 1 < n_blocks:
            s2_next = scores(qb + 1)
        o, l = _weighted_values(_exp2_weights(s2), vb_ref[k0:k0 + tk, :])
        o_ref[qb * tq:(qb + 1) * tq, :] = (o * (1.0 / l)).astype(o_ref.dtype)


def _na_bias_kernel(t_ref, o_ref, *, n_rows):
    lane = lax.broadcasted_iota(I32, (GRID_W, 2 * GRID_W), 1)
    for qb in range(n_rows // NA_Q_ROWS):
        q_row0 = qb * NA_Q_ROWS
        k_row0 = _na_window_start(q_row0, n_rows)
        for a in range(NA_Q_ROWS):
            rq = q_row0 + a
            rs = min(max(rq - NA_ROWS // 2, 0), n_rows - NA_ROWS)
            for wp in range(NA_K_ROWS // 2):
                rk = k_row0 + 2 * wp
                ok0, ok1 = rs <= rk < rs + NA_ROWS, rs <= rk + 1 < rs + NA_ROWS
                m = min(max(rk - rq + NA_ROWS, 0), 2 * NA_ROWS - 1)
                tile = t_ref[m] * LOG2E
                if not (ok0 and ok1):
                    keep = (lane < GRID_W) if ok0 else (lane >= GRID_W)
                    tile = jnp.where(keep, tile, NEG_INF) if (ok0 or ok1) else jnp.full_like(tile, NEG_INF)
                o_ref[qb, a * GRID_W:(a + 1) * GRID_W, wp * 2 * GRID_W:(wp + 1) * 2 * GRID_W] = tile


def _na_bias(rpb, n_rows):
    n_rel_r = 2 * NA_ROWS - 1
    pad = GRID_W - NA_COLS
    rp = jnp.pad(rpb, ((0, 0), (0, 0), (pad, pad)))
    toe = jnp.stack([rp[:, :, NA_COLS - 1 + pad - cq:NA_COLS - 1 + pad - cq + GRID_W] for cq in range(GRID_W)],
                    axis=2)
    cq = np.arange(GRID_W)
    cs = np.clip(cq - NA_COLS // 2, 0, GRID_W - NA_COLS)
    col_ok = (cq[None, :] >= cs[:, None]) & (cq[None, :] < cs[:, None] + NA_COLS)
    toe = jnp.where(col_ok[None, None], toe, NEG_INF)
    ext = jnp.pad(toe, ((0, 0), (1, 1), (0, 0), (0, 0)), constant_values=NEG_INF)
    pairs = jnp.concatenate([ext[:, :n_rel_r + 1], ext[:, 1:]], axis=-1)
    tq, tk = NA_Q_ROWS * GRID_W, NA_K_ROWS * GRID_W
    nqb = n_rows // NA_Q_ROWS
    return pl.pallas_call(
        functools.partial(_na_bias_kernel, n_rows=n_rows),
        grid=(N_HEADS,),
        in_specs=[pl.BlockSpec((None, n_rel_r + 1, GRID_W, 2 * GRID_W), lambda h: (h, 0, 0, 0))],
        out_specs=pl.BlockSpec((None, nqb, tq, tk), lambda h: (h, 0, 0, 0)),
        out_shape=jax.ShapeDtypeStruct((N_HEADS, nqb, tq, tk), F32),
        compiler_params=_params(1),
        name="na_bias",
    )(pairs)


def neighbourhood_attention(p, col0, g_qk, rpb):
    b, s, _ = p.shape
    bias = _na_bias(rpb, s // GRID_W)
    vec = pl.BlockSpec((1, HEAD_DIM), lambda h, bi: (0, 0))
    return pl.pallas_call(
        _na_attn_kernel,
        grid=(N_HEADS, b),
        in_specs=_head_specs(s, col0) + [
            vec, vec,
            pl.BlockSpec((None,) + bias.shape[1:], lambda h, bi: (h, 0, 0, 0))],
        out_specs=pl.BlockSpec((None, s, HEAD_DIM), lambda h, bi: (bi, 0, h)),
        out_shape=jax.ShapeDtypeStruct((b, s, GROUP_WIDTH), BF16),
        scratch_shapes=[pltpu.VMEM((s, HEAD_DIM), BF16), pltpu.VMEM((s, 2 * HEAD_DIM), BF16)],
        compiler_params=_params(2),
        name="neighbourhood_attention",
    )(p, p, p, g_qk[0].reshape(1, HEAD_DIM), g_qk[1].reshape(1, HEAD_DIM), bias)


def _mla_norm_rope(x, g, cos, sin):
    ms = jnp.sum(x * x, axis=-1, keepdims=True) * (1.0 / MLA_QK)
    xn = x * lax.rsqrt(ms + EPS) * g
    xr = xn[:, HEAD_DIM:]
    partner = pltpu.roll(xr, QK_ROPE // 2, 1) + pltpu.roll(xr, HEAD_DIM - QK_ROPE // 2, 1)
    return jnp.concatenate([xn[:, :HEAD_DIM], xr * cos + partner * sin], axis=-1)


def _mla_attn_kernel(q_ref, kn_in_ref, v_ref, kr_ref, cos_ref, sin_ref, gq_ref, gk_ref, o_ref, kf_ref, vb_ref):
    s_len = v_ref.shape[0]
    k = jnp.concatenate([kn_in_ref[...], kr_ref[...]], axis=-1)
    kf_ref[...] = _mla_norm_rope(k, gk_ref[...], cos_ref[...], sin_ref[...]).astype(BF16)
    vb_ref[...] = _values_with_ones(v_ref[...])
    def scores(r0):
        rows = slice(r0, r0 + ATT_ROWS)
        qf = (_mla_norm_rope(q_ref[rows, :], gq_ref[...], cos_ref[rows, :], sin_ref[rows, :])
              * (MLA_QK ** -0.5 * LOG2E)).astype(BF16)
        return lax.dot_general(qf, kf_ref[...], _NT, preferred_element_type=F32)

    s2_next = scores(0)
    for r0 in range(0, s_len, ATT_ROWS):
        s2 = s2_next
        if r0 + ATT_ROWS < s_len:
            s2_next = scores(r0 + ATT_ROWS)
        o, l = _weighted_values(_exp2_weights(s2), vb_ref[...])
        o_ref[r0:r0 + ATT_ROWS, :] = (o * (1.0 / l)).astype(o_ref.dtype)


def mla_attention(q_up, kv_up, k_rope, g_qk, cos_t, sin_t):
    b, s, _ = q_up.shape
    pad = MLA_QK_PAD - MLA_QK
    gq = jnp.pad(g_qk[0], (0, pad)).reshape(1, MLA_QK_PAD)
    gk = jnp.pad(g_qk[1], (0, pad)).reshape(1, MLA_QK_PAD)
    vec = pl.BlockSpec((1, MLA_QK_PAD), lambda bi, h: (0, 0))
    rope = pl.BlockSpec((s, HEAD_DIM), lambda bi, h: (0, 0))
    return pl.pallas_call(
        _mla_attn_kernel,
        grid=(b, N_HEADS),
        in_specs=[pl.BlockSpec((None, s, MLA_QK_PAD), lambda bi, h: (bi, 0, h)),
                  pl.BlockSpec((None, s, HEAD_DIM), lambda bi, h: (bi, 0, 2 * h)),
                  pl.BlockSpec((None, s, HEAD_DIM), lambda bi, h: (bi, 0, 2 * h + 1)),
                  pl.BlockSpec((None, s, HEAD_DIM), lambda bi, h: (bi, 0, 0)),
                  rope, rope, vec, vec],
        out_specs=pl.BlockSpec((None, s, HEAD_DIM), lambda bi, h: (bi, 0, h)),
        out_shape=jax.ShapeDtypeStruct((b, s, GROUP_WIDTH), BF16),
        scratch_shapes=[pltpu.VMEM((s, MLA_QK_PAD), BF16), pltpu.VMEM((s, 2 * HEAD_DIM), BF16)],
        compiler_params=_params(2),
        name="mla_attention",
    )(q_up, kv_up, kv_up, k_rope, cos_t, sin_t, gq, gk)


def _top_k_lead(s, order, payload=None):
    vals, picks = [], []
    for _ in range(PEER_TOPK):
        m = jnp.max(s, axis=0, keepdims=True)
        first = jnp.min(jnp.where(s == m, order, jnp.iinfo(jnp.int32).max), axis=0, keepdims=True)
        sel = order == first
        vals.append(m)
        picks.append(first if payload is None else jnp.max(jnp.where(sel, payload, 0), axis=0, keepdims=True))
        s = jnp.where(sel, -jnp.inf, s)
    return jnp.concatenate(vals, axis=0), jnp.concatenate(picks, axis=0)


_PAIRS = tuple((i, j) for i in range(PEER_TOPK) for j in range(PEER_TOPK) if (i + 1) * (j + 1) <= PEER_TOPK)


def _peer_route_kernel(q_ref, keys_ref, expert_ref, gate_ref):
    groups = q_ref.shape[0] // HEAD_DIM
    tile = (1, groups, HEAD_DIM)
    key_id = lax.broadcasted_iota(I32, (PEER_NKEYS,) + tile[1:], 0)
    cand_flat = jnp.concatenate([jnp.full(tile, i * PEER_TOPK + j, I32) for i, j in _PAIRS], axis=0)
    tops = []
    for hc in range(keys_ref.shape[0]):
        cols = slice(hc * HEAD_DIM, (hc + 1) * HEAD_DIM)
        sc = jnp.stack([lax.dot_general(keys_ref[hc], q_ref[g * HEAD_DIM:(g + 1) * HEAD_DIM, cols].astype(BF16),
                                        _NT, preferred_element_type=F32) for g in range(groups)], axis=0)
        tops.append(_top_k_lead(jnp.swapaxes(sc, 0, 1), key_id))
    for hh in range(keys_ref.shape[0] // 2):
        (s0, i0), (s1, i1) = tops[2 * hh], tops[2 * hh + 1]
        cand_s = jnp.concatenate([s0[i:i + 1] + s1[j:j + 1] for i, j in _PAIRS], axis=0)
        cand_e = jnp.concatenate([i0[i:i + 1] * PEER_NKEYS + i1[j:j + 1] for i, j in _PAIRS], axis=0)
        best_s, best_e = _top_k_lead(cand_s, cand_flat, cand_e)
        e = jnp.exp(best_s - jnp.max(best_s, axis=0, keepdims=True))
        gate = jnp.swapaxes(e * (1.0 / jnp.sum(e, axis=0, keepdims=True)), 0, 1)
        expert = jnp.swapaxes(best_e, 0, 1)
        rows = slice(hh * PEER_TOPK, (hh + 1) * PEER_TOPK)
        for g in range(groups):
            gate_ref[rows, g * HEAD_DIM:(g + 1) * HEAD_DIM] = gate[g]
            expert_ref[rows, g * HEAD_DIM:(g + 1) * HEAD_DIM] = expert[g]


def peer_route(q, sub_keys, tt=1024, heads_per_step=2):
    t = q.shape[0]
    n_sel = PEER_HEADS * PEER_TOPK
    hs = heads_per_step
    return pl.pallas_call(
        _peer_route_kernel,
        grid=(t // tt, PEER_HEADS // hs),
        in_specs=[pl.BlockSpec((tt, 2 * hs * HEAD_DIM), lambda i, h: (i, h)),
                  pl.BlockSpec((2 * hs, PEER_NKEYS, HEAD_DIM), lambda i, h: (h, 0, 0))],
        out_specs=[pl.BlockSpec((hs * PEER_TOPK, tt), lambda i, h: (h, i)),
                   pl.BlockSpec((hs * PEER_TOPK, tt), lambda i, h: (h, i))],
        out_shape=[jax.ShapeDtypeStruct((n_sel, t), I32), jax.ShapeDtypeStruct((n_sel, t), F32)],
        compiler_params=_params(2),
        name="peer_route",
    )(q, sub_keys)


def _peer_gate_matrix_kernel(expert_ref, gate_ref, o_ref, e_s, g_s):
    e_s[...] = expert_ref[...].T
    g_s[...] = gate_ref[...].T
    n_sel = e_s.shape[1]
    row = lax.broadcasted_iota(I32, (PEER_NKEYS, n_sel), 0)

    def body(t, carry):
        e_row = e_s[pl.ds(t, 1), :]
        g_row = g_s[pl.ds(t, 1), :]
        hi = jnp.where(row == (e_row >> 7), g_row, 0.0).astype(BF16)
        lo = jnp.where(row == (e_row & (PEER_NKEYS - 1)), 1.0, 0.0).astype(BF16)
        o_ref[t] = lax.dot_general(hi, lo, _NT, preferred_element_type=F32)
        return carry

    lax.fori_loop(0, e_s.shape[0], body, 0, unroll=16)


def peer_gate_matrix(expert_t, gate_t, tg=128):
    n_sel, t = expert_t.shape
    return pl.pallas_call(
        _peer_gate_matrix_kernel,
        grid=(t // tg,),
        in_specs=[pl.BlockSpec((n_sel, tg), lambda i: (0, i)), pl.BlockSpec((n_sel, tg), lambda i: (0, i))],
        out_specs=pl.BlockSpec((tg, PEER_NKEYS, PEER_NKEYS), lambda i: (i, 0, 0)),
        out_shape=jax.ShapeDtypeStruct((t, PEER_NKEYS, PEER_NKEYS), F32),
        scratch_shapes=[pltpu.VMEM((tg, n_sel), I32), pltpu.VMEM((tg, n_sel), F32)],
        compiler_params=_params(1),
        name="peer_gate_matrix",
    )(expert_t, gate_t)


def _rope_tables(s):
    inv_freq = 1.0 / (ROPE_THETA ** (np.arange(0, QK_ROPE, 2, dtype=np.float32) / QK_ROPE))
    ang = jnp.arange(s, dtype=F32)[:, None] * jnp.asarray(inv_freq, F32)[None, :]
    cos, sin = jnp.cos(ang), jnp.sin(ang)
    zeros = jnp.zeros((s, HEAD_DIM - QK_ROPE), F32)
    return (jnp.concatenate([cos, cos, zeros], axis=-1), jnp.concatenate([-sin, sin, zeros], axis=-1))


def _layer(x2, b, s, l, tables, g_mix, w_in, w_out, diff_lambda, diff_g_qk, diff_g_out, dil_g_qk, na_g_qk,
           na_rpb, mla_g_cq, mla_g_ckv, mla_w_uq, mla_w_ukv, mla_g_qk, g_ffn, peer_w_q, peer_sub_keys, peer_u,
           peer_v):
    t = b * s
    cos_t, sin_t, alibi, dil_tab, mult_tab = tables
    lambda_init = 0.8 - 0.6 * math.exp(-0.3 * l)
    n_qkv = 3 * QKV_COLS

    h = rms_rows(x2, g_mix)
    p = matmul(h, w_in[:, :n_qkv].astype(BF16), tm=1024).reshape(b, s, n_qkv)
    w_lat = jnp.pad(w_in[:, n_qkv:], ((0, 0), (0, HEAD_DIM - QK_ROPE))).astype(BF16)
    p_lat = matmul(h, w_lat, tn=MLA_LATENT_PAD)
    c_q, c_kv, k_rope = mla_latent_norm(p_lat, mla_g_cq, mla_g_ckv)

    blocks = QKV_COLS // HEAD_DIM
    o_a = diff_attention(p, 0, diff_lambda, diff_g_qk, diff_g_out, lambda_init, alibi)
    o_b = dilated_attention(p, blocks, dil_g_qk, dil_tab, mult_tab)
    o_c = neighbourhood_attention(p, 2 * blocks, na_g_qk, na_rpb)

    w_uq = jnp.pad(mla_w_uq.reshape(Q_LORA, N_HEADS, MLA_QK), ((0, 0), (0, 0), (0, MLA_QK_PAD - MLA_QK)))
    q_up = matmul(c_q, w_uq.reshape(Q_LORA, N_HEADS * MLA_QK_PAD).astype(BF16), tm=1024)
    kv_up = matmul(c_kv, mla_w_ukv.astype(BF16), tm=1024)
    o_d = mla_attention(q_up.reshape(b, s, -1), kv_up.reshape(b, s, -1), k_rope.reshape(b, s, HEAD_DIM),
                        mla_g_qk, cos_t, sin_t)

    mixed = jnp.concatenate([o_a, o_b, o_c, o_d], axis=-1).reshape(t, 4 * GROUP_WIDTH)
    x2 = matmul(mixed, w_out.astype(BF16), residual=x2)

    hf = rms_rows(x2, g_ffn)
    q = matmul(hf, peer_w_q.astype(BF16), tm=1024)
    keys = peer_sub_keys.reshape(2 * PEER_HEADS, PEER_NKEYS, HEAD_DIM).astype(BF16)
    expert_t, gate_t = peer_route(q, keys)
    gates = peer_gate_matrix(expert_t, gate_t)
    act = peer_gated_act(hf, peer_u.astype(BF16), gates)
    return matmul_res_ktiled(act, peer_v.astype(BF16), x2)


def kernel(x, g_mix, w_in, w_out, diff_lambda, diff_g_qk, diff_g_out, dil_g_qk, na_g_qk, na_rpb, mla_g_cq,
           mla_g_ckv, mla_w_uq, mla_w_ukv, mla_g_qk, g_ffn, peer_w_q, peer_sub_keys, peer_u, peer_v):
    b, s, d = x.shape
    depth = g_mix.shape[0]
    tables = _rope_tables(s) + tuple(alibi_tables(s))
    x2 = x.reshape(b * s, d)
    per_layer = (g_mix, w_in, w_out, diff_lambda, diff_g_qk, diff_g_out, dil_g_qk, na_g_qk, na_rpb, mla_g_cq,
                 mla_g_ckv, mla_w_uq, mla_w_ukv, mla_g_qk, g_ffn, peer_w_q, peer_sub_keys, peer_u, peer_v)
    for l in range(depth):
        x2 = _layer(x2, b, s, l, tables, *(w[l] for w in per_layer))
    return x2.reshape(b, s, d)
```

```python
import functools
import math

import jax
import jax.numpy as jnp
import numpy as np
from jax import lax
from jax.experimental import pallas as pl
from jax.experimental.pallas import tpu as pltpu

F32 = jnp.float32
BF16 = jnp.bfloat16
I32 = jnp.int32

SUBLANES = 8
HEAD_DIM = 128
N_HEADS = 8
GROUP_WIDTH = N_HEADS * HEAD_DIM
QKV_COLS = 3 * GROUP_WIDTH
EPS = 1e-6
NEG_INF = -1e30
LOG2E = math.log2(math.e)
DIFF_DH = HEAD_DIM // 2
DIL_HALF = 64
DIL_DILATIONS = (1, 4, 16)
DIL_REACH = DIL_HALF * max(DIL_DILATIONS)
GRID_W = 64
NA_ROWS = 8
NA_COLS = 16
NA_Q_ROWS = 4
NA_K_ROWS = NA_Q_ROWS + NA_ROWS
Q_LORA = 768
KV_LORA = 512
QK_NOPE = 128
QK_ROPE = 64
MLA_QK = QK_NOPE + QK_ROPE
MLA_QK_PAD = 2 * HEAD_DIM
ROPE_THETA = 10000.0
PEER_HEADS = 8
PEER_NKEYS = 128
PEER_TOPK = 16
MLA_LATENT_PAD = Q_LORA + KV_LORA + HEAD_DIM
ATT_ROWS = 256
TAB_ROWS = 512

VMEM_LIMIT = 54 * 1024 * 1024

_NT = (((1,), (1,)), ((), ()))


def _params(grid_rank):
    return pltpu.CompilerParams(dimension_semantics=("arbitrary",) * grid_rank, vmem_limit_bytes=VMEM_LIMIT)


def _rms_rows_kernel(x_ref, g_ref, o_ref):
    x = x_ref[...]
    ms = jnp.mean(x * x, axis=-1, keepdims=True)
    o_ref[...] = (x * lax.rsqrt(ms + EPS) * g_ref[...]).astype(o_ref.dtype)


def rms_rows(x, g, tm=256):
    t, c = x.shape
    return pl.pallas_call(
        _rms_rows_kernel,
        grid=(t // tm,),
        in_specs=[pl.BlockSpec((tm, c), lambda i: (i, 0)), pl.BlockSpec((1, c), lambda i: (0, 0))],
        out_specs=pl.BlockSpec((tm, c), lambda i: (i, 0)),
        out_shape=jax.ShapeDtypeStruct((t, c), BF16),
        compiler_params=_params(1),
        name="rms_rows",
    )(x, g.reshape(1, c))


def _mla_latent_kernel(p_ref, gq_ref, gkv_ref, cq_ref, ckv_ref, kr_ref):
    cq = p_ref[:, :Q_LORA]
    ckv = p_ref[:, Q_LORA:Q_LORA + KV_LORA]
    cq_ref[...] = (cq * lax.rsqrt(jnp.mean(cq * cq, axis=-1, keepdims=True) + EPS) * gq_ref[...]).astype(BF16)
    ckv_ref[...] = (ckv * lax.rsqrt(jnp.mean(ckv * ckv, axis=-1, keepdims=True) + EPS) * gkv_ref[...]).astype(BF16)
    kr_ref[...] = p_ref[:, Q_LORA + KV_LORA:]


def mla_latent_norm(p_lat, g_cq, g_ckv, tm=512):
    t = p_lat.shape[0]
    return pl.pallas_call(
        _mla_latent_kernel,
        grid=(t // tm,),
        in_specs=[pl.BlockSpec((tm, MLA_LATENT_PAD), lambda i: (i, 0)),
                  pl.BlockSpec((1, Q_LORA), lambda i: (0, 0)),
                  pl.BlockSpec((1, KV_LORA), lambda i: (0, 0))],
        out_specs=[pl.BlockSpec((tm, Q_LORA), lambda i: (i, 0)),
                   pl.BlockSpec((tm, KV_LORA), lambda i: (i, 0)),
                   pl.BlockSpec((tm, HEAD_DIM), lambda i: (i, 0))],
        out_shape=[jax.ShapeDtypeStruct((t, Q_LORA), BF16),
                   jax.ShapeDtypeStruct((t, KV_LORA), BF16),
                   jax.ShapeDtypeStruct((t, HEAD_DIM), F32)],
        compiler_params=_params(1),
        name="mla_latent_norm",
    )(p_lat, g_cq.reshape(1, Q_LORA), g_ckv.reshape(1, KV_LORA))


def _mm_kernel(a_ref, b_ref, o_ref):
    o_ref[...] = jnp.dot(a_ref[...], b_ref[...], preferred_element_type=F32).astype(o_ref.dtype)


def _mm_res_kernel(a_ref, b_ref, r_ref, o_ref):
    o_ref[...] = r_ref[...] + jnp.dot(a_ref[...], b_ref[...], preferred_element_type=F32)


def _mm_res_acc_kernel(a_ref, b_ref, r_ref, o_ref):
    @pl.when(pl.program_id(2) == 0)
    def _():
        o_ref[...] = r_ref[...]

    o_ref[...] += jnp.dot(a_ref[...], b_ref[...], preferred_element_type=F32)


def matmul(a, b, *, out_dtype=F32, residual=None, tm=512, tn=1024):
    m, kd = a.shape
    n = b.shape[1]
    tm, tn = min(tm, m), min(tn, n)
    in_specs = [pl.BlockSpec((tm, kd), lambda j, i: (i, 0)),
                pl.BlockSpec((kd, tn), lambda j, i: (0, j))]
    args = [a, b]
    body = _mm_kernel
    if residual is not None:
        body = _mm_res_kernel
        in_specs.append(pl.BlockSpec((tm, tn), lambda j, i: (i, j)))
        args.append(residual)
    return pl.pallas_call(
        body,
        grid=(n // tn, m // tm),
        in_specs=in_specs,
        out_specs=pl.BlockSpec((tm, tn), lambda j, i: (i, j)),
        out_shape=jax.ShapeDtypeStruct((m, n), out_dtype),
        compiler_params=_params(2),
        name="matmul_res" if residual is not None else "matmul",
    )(*args)


def matmul_res_ktiled(a, b, residual, tm=1024, tn=1024, tk=2048):
    m, kd = a.shape
    n = b.shape[1]
    return pl.pallas_call(
        _mm_res_acc_kernel,
        grid=(m // tm, n // tn, kd // tk),
        in_specs=[pl.BlockSpec((tm, tk), lambda i, j, k: (i, k)),
                  pl.BlockSpec((tk, tn), lambda i, j, k: (k, j)),
                  pl.BlockSpec((tm, tn), lambda i, j, k: (i, j))],
        out_specs=pl.BlockSpec((tm, tn), lambda i, j, k: (i, j)),
        out_shape=jax.ShapeDtypeStruct((m, n), F32),
        compiler_params=_params(3),
        name="matmul_res_ktiled",
    )(a, b, residual)


def _gelu_exact(x):
    return 0.5 * x * (1.0 + lax.erf(x * (1.0 / math.sqrt(2.0))))


def _peer_act_kernel(h_ref, u_ref, g_ref, o_ref):
    tm, n_key0, n_key1 = g_ref.shape
    g = jnp.swapaxes(g_ref[...].reshape(tm // SUBLANES, SUBLANES, n_key0, n_key1), 1, 2)
    act = _gelu_exact(lax.dot_general(h_ref[...], u_ref[...], _NT, preferred_element_type=F32))
    for a in range(n_key0):
        cols = slice(a * n_key1, (a + 1) * n_key1)
        o_ref[:, cols] = (act[:, cols] * g[:, a].reshape(tm, n_key1)).astype(o_ref.dtype)


def peer_gated_act(hn, u, gates, tm=1024, tn=1024):
    t, d = hn.shape
    e = u.shape[0]
    tm = min(tm, t)
    n_key0 = tn // PEER_NKEYS
    return pl.pallas_call(
        _peer_act_kernel,
        grid=(e // tn, t // tm),
        in_specs=[pl.BlockSpec((tm, d), lambda j, i: (i, 0)),
                  pl.BlockSpec((tn, d), lambda j, i: (j, 0)),
                  pl.BlockSpec((tm, n_key0, PEER_NKEYS), lambda j, i: (i, j, 0))],
        out_specs=pl.BlockSpec((tm, tn), lambda j, i: (i, j)),
        out_shape=jax.ShapeDtypeStruct((t, e), BF16),
        compiler_params=_params(2),
        name="peer_gated_act",
    )(hn, u, gates)


def _rms_lanes(x, g):
    ms = jnp.mean(x * x, axis=-1, keepdims=True)
    return x * lax.rsqrt(ms + EPS) * g


def _rms_halves(x, g):
    lane = lax.broadcasted_iota(I32, x.shape, 1)
    lo = lane < DIFF_DH
    xx = x * x
    ms_lo = jnp.sum(jnp.where(lo, xx, 0.0), axis=-1, keepdims=True) * (1.0 / DIFF_DH)
    ms_hi = jnp.sum(jnp.where(lo, 0.0, xx), axis=-1, keepdims=True) * (1.0 / DIFF_DH)
    inv = jnp.where(lo, lax.rsqrt(ms_lo + EPS), lax.rsqrt(ms_hi + EPS))
    return x * inv * g


def _exp2_weights(s2):
    return jnp.exp2(s2 - jnp.max(s2, axis=-1, keepdims=True))


def _values_with_ones(v):
    ones_col = jnp.where(lax.broadcasted_iota(I32, v.shape, 1) == 0, 1.0, 0.0)
    return jnp.concatenate([v, ones_col], axis=-1).astype(BF16)


def _weighted_values(p, vx):
    ov = jnp.dot(p.astype(BF16), vx, preferred_element_type=F32)
    return ov[:, :HEAD_DIM], ov[:, HEAD_DIM:HEAD_DIM + 1]


def _alibi_tables_kernel(slope_ref, alibi_ref, dil_ref, mult_ref, *, s_len):
    shape = alibi_ref.shape
    d = lax.broadcasted_iota(I32, shape, 1) - lax.broadcasted_iota(I32, shape, 0) - (s_len - TAB_ROWS)
    ad = jnp.abs(d)
    alibi = (-LOG2E * slope_ref[pl.program_id(0)]) * ad.astype(F32)
    mult = jnp.zeros(shape, F32)
    for dil in DIL_DILATIONS:
        mult = mult + jnp.where(((d & (dil - 1)) == 0) & (ad <= DIL_HALF * dil), 1.0, 0.0)
    alibi_ref[...] = alibi
    dil_ref[...] = jnp.where(mult > 0.0, alibi, NEG_INF)
    mult_ref[...] = mult


def alibi_tables(s_len):
    slopes = jnp.asarray(2.0 ** (-8.0 * np.arange(1, N_HEADS + 1) / N_HEADS), F32)
    width = 2 * s_len - TAB_ROWS
    per_head = pl.BlockSpec((None, TAB_ROWS, width), lambda h: (h, 0, 0))
    return pl.pallas_call(
        functools.partial(_alibi_tables_kernel, s_len=s_len),
        grid=(N_HEADS,),
        in_specs=[pl.BlockSpec(memory_space=pltpu.SMEM)],
        out_specs=[per_head, per_head, pl.BlockSpec((TAB_ROWS, width), lambda h: (0, 0))],
        out_shape=[jax.ShapeDtypeStruct((N_HEADS, TAB_ROWS, width), F32),
                   jax.ShapeDtypeStruct((N_HEADS, TAB_ROWS, width), F32),
                   jax.ShapeDtypeStruct((TAB_ROWS, width), F32)],
        compiler_params=_params(1),
        name="alibi_tables",
    )(slopes)


def _table_tile(tab_ref, r0, k_lo, k_hi, s_len):
    row = r0 % TAB_ROWS
    lane0 = s_len - TAB_ROWS - (r0 - row) + k_lo
    return tab_ref[row:row + ATT_ROWS, lane0:lane0 + (k_hi - k_lo)]


def _head_specs(s, col0, width=HEAD_DIM):
    return [pl.BlockSpec((None, s, width), lambda h, bi, c=c: (bi, 0, col0 + c * N_HEADS + h)) for c in range(3)]


def _diff_attn_kernel(q_ref, k_ref, v_ref, gq_ref, gk_ref, go_ref, lam_ref, tab_ref, o_ref, kn_ref, vb_ref,
                      *, lambda_init):
    s_len = k_ref.shape[0]
    kn_ref[...] = _rms_halves(k_ref[...], gk_ref[...]).astype(BF16)
    vb_ref[...] = _values_with_ones(v_ref[...])
    lv = lam_ref[...]
    lam = (jnp.exp(jnp.sum(lv[0:1] * lv[1:2], axis=-1, keepdims=True))
           - jnp.exp(jnp.sum(lv[2:3] * lv[3:4], axis=-1, keepdims=True)) + lambda_init)
    lo = lax.broadcasted_iota(I32, (ATT_ROWS, HEAD_DIM), 1) < DIFF_DH
    def scores(r0):
        qn = _rms_halves(q_ref[r0:r0 + ATT_ROWS, :], gq_ref[...]) * (DIFF_DH ** -0.5 * LOG2E)
        tab = _table_tile(tab_ref, r0, 0, s_len, s_len)
        return tuple(lax.dot_general(qc.astype(BF16), kn_ref[...], _NT, preferred_element_type=F32) + tab
                     for qc in (jnp.where(lo, qn, 0.0), jnp.where(lo, 0.0, qn)))

    s2_next = scores(0)
    for r0 in range(0, s_len, ATT_ROWS):
        s2 = s2_next
        if r0 + ATT_ROWS < s_len:
            s2_next = scores(r0 + ATT_ROWS)
        o0, l0 = _weighted_values(_exp2_weights(s2[0]), vb_ref[...])
        o1, l1 = _weighted_values(_exp2_weights(s2[1]), vb_ref[...])
        o = o0 * (1.0 / l0) - o1 * (lam / l1)
        o_ref[r0:r0 + ATT_ROWS, :] = (_rms_lanes(o, go_ref[...]) * (1.0 - lambda_init)).astype(o_ref.dtype)


def diff_attention(p, col0, lam_vecs, g_qk, g_out, lambda_init, alibi):
    b, s, _ = p.shape
    lam_pad = jnp.pad(lam_vecs, ((0, 0), (0, HEAD_DIM - DIFF_DH)))
    gq = jnp.tile(g_qk[0], 2).reshape(1, HEAD_DIM)
    gk = jnp.tile(g_qk[1], 2).reshape(1, HEAD_DIM)
    vec = pl.BlockSpec((1, HEAD_DIM), lambda h, bi: (0, 0))
    return pl.pallas_call(
        functools.partial(_diff_attn_kernel, lambda_init=lambda_init),
        grid=(N_HEADS, b),
        in_specs=_head_specs(s, col0) + [
            vec, vec, vec,
            pl.BlockSpec((4, HEAD_DIM), lambda h, bi: (0, 0)),
            pl.BlockSpec((None,) + alibi.shape[1:], lambda h, bi: (h, 0, 0))],
        out_specs=pl.BlockSpec((None, s, HEAD_DIM), lambda h, bi: (bi, 0, h)),
        out_shape=jax.ShapeDtypeStruct((b, s, GROUP_WIDTH), BF16),
        scratch_shapes=[pltpu.VMEM((s, HEAD_DIM), BF16), pltpu.VMEM((s, 2 * HEAD_DIM), BF16)],
        compiler_params=_params(2),
        name="diff_attention",
    )(p, p, p, gq, gk, g_out.reshape(1, HEAD_DIM), lam_pad, alibi)


def _dilated_attn_kernel(q_ref, k_ref, v_ref, gq_ref, gk_ref, tab_ref, mult_ref, o_ref, kn_ref, vb_ref):
    s_len = k_ref.shape[0]
    kn_ref[...] = _rms_lanes(k_ref[...], gk_ref[...]).astype(BF16)
    vb_ref[...] = _values_with_ones(v_ref[...])
    def key_window(r0):
        return max(0, r0 - DIL_REACH), min(s_len, r0 + ATT_ROWS + DIL_REACH)

    def scores(r0):
        k_lo, k_hi = key_window(r0)
        qn = (_rms_lanes(q_ref[r0:r0 + ATT_ROWS, :], gq_ref[...]) * (HEAD_DIM ** -0.5 * LOG2E)).astype(BF16)
        return (lax.dot_general(qn, kn_ref[k_lo:k_hi, :], _NT, preferred_element_type=F32)
                + _table_tile(tab_ref, r0, k_lo, k_hi, s_len))

    s2_next = scores(0)
    for r0 in range(0, s_len, ATT_ROWS):
        k_lo, k_hi = key_window(r0)
        s2 = s2_next
        if r0 + ATT_ROWS < s_len:
            s2_next = scores(r0 + ATT_ROWS)
        o, l = _weighted_values(_exp2_weights(s2) * _table_tile(mult_ref, r0, k_lo, k_hi, s_len),
                                vb_ref[k_lo:k_hi, :])
        o_ref[r0:r0 + ATT_ROWS, :] = (o * (1.0 / l)).astype(o_ref.dtype)


def dilated_attention(p, col0, g_qk, dil_tab, mult_tab):
    b, s, _ = p.shape
    vec = pl.BlockSpec((1, HEAD_DIM), lambda h, bi: (0, 0))
    return pl.pallas_call(
        _dilated_attn_kernel,
        grid=(N_HEADS, b),
        in_specs=_head_specs(s, col0) + [
            vec, vec,
            pl.BlockSpec((None,) + dil_tab.shape[1:], lambda h, bi: (h, 0, 0)),
            pl.BlockSpec(mult_tab.shape, lambda h, bi: (0, 0))],
        out_specs=pl.BlockSpec((None, s, HEAD_DIM), lambda h, bi: (bi, 0, h)),
        out_shape=jax.ShapeDtypeStruct((b, s, GROUP_WIDTH), BF16),
        scratch_shapes=[pltpu.VMEM((s, HEAD_DIM), BF16), pltpu.VMEM((s, 2 * HEAD_DIM), BF16)],
        compiler_params=_params(2),
        name="dilated_attention",
    )(p, p, p, g_qk[0].reshape(1, HEAD_DIM), g_qk[1].reshape(1, HEAD_DIM), dil_tab, mult_tab)


def _na_window_start(q_row0, n_rows):
    return min(max(q_row0 - NA_ROWS // 2, 0), n_rows - NA_K_ROWS)


def _na_attn_kernel(q_ref, k_ref, v_ref, gq_ref, gk_ref, bias_ref, o_ref, kn_ref, vb_ref):
    n_rows = k_ref.shape[0] // GRID_W
    kn_ref[...] = _rms_lanes(k_ref[...], gk_ref[...]).astype(BF16)
    vb_ref[...] = _values_with_ones(v_ref[...])
    tq, tk = NA_Q_ROWS * GRID_W, NA_K_ROWS * GRID_W
    n_blocks = n_rows // NA_Q_ROWS

    def scores(qb):
        k0 = _na_window_start(qb * NA_Q_ROWS, n_rows) * GRID_W
        qn = (_rms_lanes(q_ref[qb * tq:(qb + 1) * tq, :], gq_ref[...]) * (HEAD_DIM ** -0.5 * LOG2E)).astype(BF16)
        return lax.dot_general(qn, kn_ref[k0:k0 + tk, :], _NT, preferred_element_type=F32) + bias_ref[qb]

    s2_next = scores(0)
    for qb in range(n_blocks):
        k0 = _na_window_start(qb * NA_Q_ROWS, n_rows) * GRID_W
        s2 = s2_next
        if qb + 1 < n_blocks:
            s2_next = scores(qb + 1)
        o, l = _weighted_values(_exp2_weights(s2), vb_ref[k0:k0 + tk, :])
        o_ref[qb * tq:(qb + 1) * tq, :] = (o * (1.0 / l)).astype(o_ref.dtype)


def _na_bias_kernel(t_ref, o_ref, *, n_rows):
    lane = lax.broadcasted_iota(I32, (GRID_W, 2 * GRID_W), 1)
    for qb in range(n_rows // NA_Q_ROWS):
        q_row0 = qb * NA_Q_ROWS
        k_row0 = _na_window_start(q_row0, n_rows)
        for a in range(NA_Q_ROWS):
            rq = q_row0 + a
            rs = min(max(rq - NA_ROWS // 2, 0), n_rows - NA_ROWS)
            for wp in range(NA_K_ROWS // 2):
                rk = k_row0 + 2 * wp
                ok0, ok1 = rs <= rk < rs + NA_ROWS, rs <= rk + 1 < rs + NA_ROWS
                m = min(max(rk - rq + NA_ROWS, 0), 2 * NA_ROWS - 1)
                tile = t_ref[m] * LOG2E
                if not (ok0 and ok1):
                    keep = (lane < GRID_W) if ok0 else (lane >= GRID_W)
                    tile = jnp.where(keep, tile, NEG_INF) if (ok0 or ok1) else jnp.full_like(tile, NEG_INF)
                o_ref[qb, a * GRID_W:(a + 1) * GRID_W, wp * 2 * GRID_W:(wp + 1) * 2 * GRID_W] = tile


def _na_bias(rpb, n_rows):
    n_rel_r = 2 * NA_ROWS - 1
    pad = GRID_W - NA_COLS
    rp = jnp.pad(rpb, ((0, 0), (0, 0), (pad, pad)))
    toe = jnp.stack([rp[:, :, NA_COLS - 1 + pad - cq:NA_COLS - 1 + pad - cq + GRID_W] for cq in range(GRID_W)],
                    axis=2)
    cq = np.arange(GRID_W)
    cs = np.clip(cq - NA_COLS // 2, 0, GRID_W - NA_COLS)
    col_ok = (cq[None, :] >= cs[:, None]) & (cq[None, :] < cs[:, None] + NA_COLS)
    toe = jnp.where(col_ok[None, None], toe, NEG_INF)
    ext = jnp.pad(toe, ((0, 0), (1, 1), (0, 0), (0, 0)), constant_values=NEG_INF)
    pairs = jnp.concatenate([ext[:, :n_rel_r + 1], ext[:, 1:]], axis=-1)
    tq, tk = NA_Q_ROWS * GRID_W, NA_K_ROWS * GRID_W
    nqb = n_rows // NA_Q_ROWS
    return pl.pallas_call(
        functools.partial(_na_bias_kernel, n_rows=n_rows),
        grid=(N_HEADS,),
        in_specs=[pl.BlockSpec((None, n_rel_r + 1, GRID_W, 2 * GRID_W), lambda h: (h, 0, 0, 0))],
        out_specs=pl.BlockSpec((None, nqb, tq, tk), lambda h: (h, 0, 0, 0)),
        out_shape=jax.ShapeDtypeStruct((N_HEADS, nqb, tq, tk), F32),
        compiler_params=_params(1),
        name="na_bias",
    )(pairs)


def neighbourhood_attention(p, col0, g_qk, rpb):
    b, s, _ = p.shape
    bias = _na_bias(rpb, s // GRID_W)
    vec = pl.BlockSpec((1, HEAD_DIM), lambda h, bi: (0, 0))
    return pl.pallas_call(
        _na_attn_kernel,
        grid=(N_HEADS, b),
        in_specs=_head_specs(s, col0) + [
            vec, vec,
            pl.BlockSpec((None,) + bias.shape[1:], lambda h, bi: (h, 0, 0, 0))],
        out_specs=pl.BlockSpec((None, s, HEAD_DIM), lambda h, bi: (bi, 0, h)),
        out_shape=jax.ShapeDtypeStruct((b, s, GROUP_WIDTH), BF16),
        scratch_shapes=[pltpu.VMEM((s, HEAD_DIM), BF16), pltpu.VMEM((s, 2 * HEAD_DIM), BF16)],
        compiler_params=_params(2),
        name="neighbourhood_attention",
    )(p, p, p, g_qk[0].reshape(1, HEAD_DIM), g_qk[1].reshape(1, HEAD_DIM), bias)


def _mla_norm_rope(x, g, cos, sin):
    ms = jnp.sum(x * x, axis=-1, keepdims=True) * (1.0 / MLA_QK)
    xn = x * lax.rsqrt(ms + EPS) * g
    xr = xn[:, HEAD_DIM:]
    partner = pltpu.roll(xr, QK_ROPE // 2, 1) + pltpu.roll(xr, HEAD_DIM - QK_ROPE // 2, 1)
    return jnp.concatenate([xn[:, :HEAD_DIM], xr * cos + partner * sin], axis=-1)


def _mla_attn_kernel(q_ref, kn_in_ref, v_ref, kr_ref, cos_ref, sin_ref, gq_ref, gk_ref, o_ref, kf_ref, vb_ref):
    s_len = v_ref.shape[0]
    k = jnp.concatenate([kn_in_ref[...], kr_ref[...]], axis=-1)
    kf_ref[...] = _mla_norm_rope(k, gk_ref[...], cos_ref[...], sin_ref[...]).astype(BF16)
    vb_ref[...] = _values_with_ones(v_ref[...])
    def scores(r0):
        rows = slice(r0, r0 + ATT_ROWS)
        qf = (_mla_norm_rope(q_ref[rows, :], gq_ref[...], cos_ref[rows, :], sin_ref[rows, :])
              * (MLA_QK ** -0.5 * LOG2E)).astype(BF16)
        return lax.dot_general(qf, kf_ref[...], _NT, preferred_element_type=F32)

    s2_next = scores(0)
    for r0 in range(0, s_len, ATT_ROWS):
        s2 = s2_next
        if r0 + ATT_ROWS < s_len:
            s2_next = scores(r0 + ATT_ROWS)
        o, l = _weighted_values(_exp2_weights(s2), vb_ref[...])
        o_ref[r0:r0 + ATT_ROWS, :] = (o * (1.0 / l)).astype(o_ref.dtype)


def mla_attention(q_up, kv_up, k_rope, g_qk, cos_t, sin_t):
    b, s, _ = q_up.shape
    pad = MLA_QK_PAD - MLA_QK
    gq = jnp.pad(g_qk[0], (0, pad)).reshape(1, MLA_QK_PAD)
    gk = jnp.pad(g_qk[1], (0, pad)).reshape(1, MLA_QK_PAD)
    vec = pl.BlockSpec((1, MLA_QK_PAD), lambda bi, h: (0, 0))
    rope = pl.BlockSpec((s, HEAD_DIM), lambda bi, h: (0, 0))
    return pl.pallas_call(
        _mla_attn_kernel,
        grid=(b, N_HEADS),
        in_specs=[pl.BlockSpec((None, s, MLA_QK_PAD), lambda bi, h: (bi, 0, h)),
                  pl.BlockSpec((None, s, HEAD_DIM), lambda bi, h: (bi, 0, 2 * h)),
                  pl.BlockSpec((None, s, HEAD_DIM), lambda bi, h: (bi, 0, 2 * h + 1)),
                  pl.BlockSpec((None, s, HEAD_DIM), lambda bi, h: (bi, 0, 0)),
                  rope, rope, vec, vec],
        out_specs=pl.BlockSpec((None, s, HEAD_DIM), lambda bi, h: (bi, 0, h)),
        out_shape=jax.ShapeDtypeStruct((b, s, GROUP_WIDTH), BF16),
        scratch_shapes=[pltpu.VMEM((s, MLA_QK_PAD), BF16), pltpu.VMEM((s, 2 * HEAD_DIM), BF16)],
        compiler_params=_params(2),
        name="mla_attention",
    )(q_up, kv_up, kv_up, k_rope, cos_t, sin_t, gq, gk)


def _top_k_lead(s, order, payload=None):
    vals, picks = [], []
    for _ in range(PEER_TOPK):
        m = jnp.max(s, axis=0, keepdims=True)
        first = jnp.min(jnp.where(s == m, order, jnp.iinfo(jnp.int32).max), axis=0, keepdims=True)
        sel = order == first
        vals.append(m)
        picks.append(first if payload is None else jnp.max(jnp.where(sel, payload, 0), axis=0, keepdims=True))
        s = jnp.where(sel, -jnp.inf, s)
    return jnp.concatenate(vals, axis=0), jnp.concatenate(picks, axis=0)


_PAIRS = tuple((i, j) for i in range(PEER_TOPK) for j in range(PEER_TOPK) if (i + 1) * (j + 1) <= PEER_TOPK)


def _peer_route_kernel(q_ref, keys_ref, expert_ref, gate_ref):
    groups = q_ref.shape[0] // HEAD_DIM
    tile = (1, groups, HEAD_DIM)
    key_id = lax.broadcasted_iota(I32, (PEER_NKEYS,) + tile[1:], 0)
    cand_flat = jnp.concatenate([jnp.full(tile, i * PEER_TOPK + j, I32) for i, j in _PAIRS], axis=0)
    tops = []
    for hc in range(keys_ref.shape[0]):
        cols = slice(hc * HEAD_DIM, (hc + 1) * HEAD_DIM)
        sc = jnp.stack([lax.dot_general(keys_ref[hc], q_ref[g * HEAD_DIM:(g + 1) * HEAD_DIM, cols].astype(BF16),
                                        _NT, preferred_element_type=F32) for g in range(groups)], axis=0)
        tops.append(_top_k_lead(jnp.swapaxes(sc, 0, 1), key_id))
    for hh in range(keys_ref.shape[0] // 2):
        (s0, i0), (s1, i1) = tops[2 * hh], tops[2 * hh + 1]
        cand_s = jnp.concatenate([s0[i:i + 1] + s1[j:j + 1] for i, j in _PAIRS], axis=0)
        cand_e = jnp.concatenate([i0[i:i + 1] * PEER_NKEYS + i1[j:j + 1] for i, j in _PAIRS], axis=0)
        best_s, best_e = _top_k_lead(cand_s, cand_flat, cand_e)
        e = jnp.exp(best_s - jnp.max(best_s, axis=0, keepdims=True))
        gate = jnp.swapaxes(e * (1.0 / jnp.sum(e, axis=0, keepdims=True)), 0, 1)
        expert = jnp.swapaxes(best_e, 0, 1)
        rows = slice(hh * PEER_TOPK, (hh + 1) * PEER_TOPK)
        for g in range(groups):
            gate_ref[rows, g * HEAD_DIM:(g + 1) * HEAD_DIM] = gate[g]
            expert_ref[rows, g * HEAD_DIM:(g + 1) * HEAD_DIM] = expert[g]


def peer_route(q, sub_keys, tt=1024, heads_per_step=2):
    t = q.shape[0]
    n_sel = PEER_HEADS * PEER_TOPK
    hs = heads_per_step
    return pl.pallas_call(
        _peer_route_kernel,
        grid=(t // tt, PEER_HEADS // hs),
        in_specs=[pl.BlockSpec((tt, 2 * hs * HEAD_DIM), lambda i, h: (i, h)),
                  pl.BlockSpec((2 * hs, PEER_NKEYS, HEAD_DIM), lambda i, h: (h, 0, 0))],
        out_specs=[pl.BlockSpec((hs * PEER_TOPK, tt), lambda i, h: (h, i)),
                   pl.BlockSpec((hs * PEER_TOPK, tt), lambda i, h: (h, i))],
        out_shape=[jax.ShapeDtypeStruct((n_sel, t), I32), jax.ShapeDtypeStruct((n_sel, t), F32)],
        compiler_params=_params(2),
        name="peer_route",
    )(q, sub_keys)


def _peer_gate_matrix_kernel(expert_ref, gate_ref, o_ref, e_s, g_s):
    e_s[...] = expert_ref[...].T
    g_s[...] = gate_ref[...].T
    n_sel = e_s.shape[1]
    row = lax.broadcasted_iota(I32, (PEER_NKEYS, n_sel), 0)

    for t in range(e_s.shape[0]):
        e_row = e_s[t:t + 1, :]
        g_row = g_s[t:t + 1, :]
        hi = jnp.where(row == (e_row >> 7), g_row, 0.0).astype(BF16)
        lo = jnp.where(row == (e_row & (PEER_NKEYS - 1)), 1.0, 0.0).astype(BF16)
        o_ref[t] = lax.dot_general(hi, lo, _NT, preferred_element_type=F32)


def peer_gate_matrix(expert_t, gate_t, tg=128):
    n_sel, t = expert_t.shape
    return pl.pallas_call(
        _peer_gate_matrix_kernel,
        grid=(t // tg,),
        in_specs=[pl.BlockSpec((n_sel, tg), lambda i: (0, i)), pl.BlockSpec((n_sel, tg), lambda i: (0, i))],
        out_specs=pl.BlockSpec((tg, PEER_NKEYS, PEER_NKEYS), lambda i: (i, 0, 0)),
        out_shape=jax.ShapeDtypeStruct((t, PEER_NKEYS, PEER_NKEYS), F32),
        scratch_shapes=[pltpu.VMEM((tg, n_sel), I32), pltpu.VMEM((tg, n_sel), F32)],
        compiler_params=_params(1),
        name="peer_gate_matrix",
    )(expert_t, gate_t)


def _rope_tables(s):
    inv_freq = 1.0 / (ROPE_THETA ** (np.arange(0, QK_ROPE, 2, dtype=np.float32) / QK_ROPE))
    ang = jnp.arange(s, dtype=F32)[:, None] * jnp.asarray(inv_freq, F32)[None, :]
    cos, sin = jnp.cos(ang), jnp.sin(ang)
    zeros = jnp.zeros((s, HEAD_DIM - QK_ROPE), F32)
    return (jnp.concatenate([cos, cos, zeros], axis=-1), jnp.concatenate([-sin, sin, zeros], axis=-1))


def _layer(x2, b, s, l, tables, g_mix, w_in, w_out, diff_lambda, diff_g_qk, diff_g_out, dil_g_qk, na_g_qk,
           na_rpb, mla_g_cq, mla_g_ckv, mla_w_uq, mla_w_ukv, mla_g_qk, g_ffn, peer_w_q, peer_sub_keys, peer_u,
           peer_v):
    t = b * s
    cos_t, sin_t, alibi, dil_tab, mult_tab = tables
    lambda_init = 0.8 - 0.6 * math.exp(-0.3 * l)
    n_qkv = 3 * QKV_COLS

    h = rms_rows(x2, g_mix)
    p = matmul(h, w_in[:, :n_qkv].astype(BF16), tm=1024).reshape(b, s, n_qkv)
    w_lat = jnp.pad(w_in[:, n_qkv:], ((0, 0), (0, HEAD_DIM - QK_ROPE))).astype(BF16)
    p_lat = matmul(h, w_lat, tn=MLA_LATENT_PAD)
    c_q, c_kv, k_rope = mla_latent_norm(p_lat, mla_g_cq, mla_g_ckv)

    blocks = QKV_COLS // HEAD_DIM
    o_a = diff_attention(p, 0, diff_lambda, diff_g_qk, diff_g_out, lambda_init, alibi)
    o_b = dilated_attention(p, blocks, dil_g_qk, dil_tab, mult_tab)
    o_c = neighbourhood_attention(p, 2 * blocks, na_g_qk, na_rpb)

    w_uq = jnp.pad(mla_w_uq.reshape(Q_LORA, N_HEADS, MLA_QK), ((0, 0), (0, 0), (0, MLA_QK_PAD - MLA_QK)))
    q_up = matmul(c_q, w_uq.reshape(Q_LORA, N_HEADS * MLA_QK_PAD).astype(BF16), tm=1024)
    kv_up = matmul(c_kv, mla_w_ukv.astype(BF16), tm=1024)
    o_d = mla_attention(q_up.reshape(b, s, -1), kv_up.reshape(b, s, -1), k_rope.reshape(b, s, HEAD_DIM),
                        mla_g_qk, cos_t, sin_t)

    mixed = jnp.concatenate([o_a, o_b, o_c, o_d], axis=-1).reshape(t, 4 * GROUP_WIDTH)
    x2 = matmul(mixed, w_out.astype(BF16), residual=x2)

    hf = rms_rows(x2, g_ffn)
    q = matmul(hf, peer_w_q.astype(BF16), tm=1024)
    keys = peer_sub_keys.reshape(2 * PEER_HEADS, PEER_NKEYS, HEAD_DIM).astype(BF16)
    expert_t, gate_t = peer_route(q, keys)
    gates = peer_gate_matrix(expert_t, gate_t)
    act = peer_gated_act(hf, peer_u.astype(BF16), gates)
    return matmul_res_ktiled(act, peer_v.astype(BF16), x2)


def kernel(x, g_mix, w_in, w_out, diff_lambda, diff_g_qk, diff_g_out, dil_g_qk, na_g_qk, na_rpb, mla_g_cq,
           mla_g_ckv, mla_w_uq, mla_w_ukv, mla_g_qk, g_ffn, peer_w_q, peer_sub_keys, peer_u, peer_v):
    b, s, d = x.shape
    depth = g_mix.shape[0]
    tables = _rope_tables(s) + tuple(alibi_tables(s))
    x2 = x.reshape(b * s, d)
    per_layer = (g_mix, w_in, w_out, diff_lambda, diff_g_qk, diff_g_out, dil_g_qk, na_g_qk, na_rpb, mla_g_cq,
                 mla_g_ckv, mla_w_uq, mla_w_ukv, mla_g_qk, g_ffn, peer_w_q, peer_sub_keys, peer_u, peer_v)
    for l in range(depth):
        x2 = _layer(x2, b, s, l, tables, *(w[l] for w in per_layer))
    return x2.reshape(b, s, d)
```

```python
import functools
import math

import jax
import jax.numpy as jnp
import numpy as np
from jax import lax
from jax.experimental import pallas as pl
from jax.experimental.pallas import tpu as pltpu

F32 = jnp.float32
BF16 = jnp.bfloat16
I32 = jnp.int32

SUBLANES = 8
HEAD_DIM = 128
N_HEADS = 8
GROUP_WIDTH = N_HEADS * HEAD_DIM
QKV_COLS = 3 * GROUP_WIDTH
EPS = 1e-6
NEG_INF = -1e30
LOG2E = math.log2(math.e)
DIFF_DH = HEAD_DIM // 2
DIL_HALF = 64
DIL_DILATIONS = (1, 4, 16)
DIL_REACH = DIL_HALF * max(DIL_DILATIONS)
GRID_W = 64
NA_ROWS = 8
NA_COLS = 16
NA_Q_ROWS = 4
NA_K_ROWS = NA_Q_ROWS + NA_ROWS
Q_LORA = 768
KV_LORA = 512
QK_NOPE = 128
QK_ROPE = 64
MLA_QK = QK_NOPE + QK_ROPE
MLA_QK_PAD = 2 * HEAD_DIM
ROPE_THETA = 10000.0
PEER_HEADS = 8
PEER_NKEYS = 128
PEER_TOPK = 16
MLA_LATENT_PAD = Q_LORA + KV_LORA + HEAD_DIM
ATT_ROWS = 256
TAB_ROWS = 512

VMEM_LIMIT = 54 * 1024 * 1024

_NT = (((1,), (1,)), ((), ()))


def _params(grid_rank):
    return pltpu.CompilerParams(dimension_semantics=("arbitrary",) * grid_rank, vmem_limit_bytes=VMEM_LIMIT)


def _rms_rows_kernel(x_ref, g_ref, o_ref):
    x = x_ref[...]
    ms = jnp.mean(x * x, axis=-1, keepdims=True)
    o_ref[...] = (x * lax.rsqrt(ms + EPS) * g_ref[...]).astype(o_ref.dtype)


def rms_rows(x, g, tm=256):
    t, c = x.shape
    return pl.pallas_call(
        _rms_rows_kernel,
        grid=(t // tm,),
        in_specs=[pl.BlockSpec((tm, c), lambda i: (i, 0)), pl.BlockSpec((1, c), lambda i: (0, 0))],
        out_specs=pl.BlockSpec((tm, c), lambda i: (i, 0)),
        out_shape=jax.ShapeDtypeStruct((t, c), BF16),
        compiler_params=_params(1),
        name="rms_rows",
    )(x, g.reshape(1, c))


def _mla_latent_kernel(p_ref, gq_ref, gkv_ref, cq_ref, ckv_ref, kr_ref):
    cq = p_ref[:, :Q_LORA]
    ckv = p_ref[:, Q_LORA:Q_LORA + KV_LORA]
    cq_ref[...] = (cq * lax.rsqrt(jnp.mean(cq * cq, axis=-1, keepdims=True) + EPS) * gq_ref[...]).astype(BF16)
    ckv_ref[...] = (ckv * lax.rsqrt(jnp.mean(ckv * ckv, axis=-1, keepdims=True) + EPS) * gkv_ref[...]).astype(BF16)
    kr_ref[...] = p_ref[:, Q_LORA + KV_LORA:]


def mla_latent_norm(p_lat, g_cq, g_ckv, tm=512):
    t = p_lat.shape[0]
    return pl.pallas_call(
        _mla_latent_kernel,
        grid=(t // tm,),
        in_specs=[pl.BlockSpec((tm, MLA_LATENT_PAD), lambda i: (i, 0)),
                  pl.BlockSpec((1, Q_LORA), lambda i: (0, 0)),
                  pl.BlockSpec((1, KV_LORA), lambda i: (0, 0))],
        out_specs=[pl.BlockSpec((tm, Q_LORA), lambda i: (i, 0)),
                   pl.BlockSpec((tm, KV_LORA), lambda i: (i, 0)),
                   pl.BlockSpec((tm, HEAD_DIM), lambda i: (i, 0))],
        out_shape=[jax.ShapeDtypeStruct((t, Q_LORA), BF16),
                   jax.ShapeDtypeStruct((t, KV_LORA), BF16),
                   jax.ShapeDtypeStruct((t, HEAD_DIM), F32)],
        compiler_params=_params(1),
        name="mla_latent_norm",
    )(p_lat, g_cq.reshape(1, Q_LORA), g_ckv.reshape(1, KV_LORA))


def _mm_kernel(a_ref, b_ref, o_ref):
    o_ref[...] = jnp.dot(a_ref[...], b_ref[...], preferred_element_type=F32).astype(o_ref.dtype)


def _mm_res_kernel(a_ref, b_ref, r_ref, o_ref):
    o_ref[...] = r_ref[...] + jnp.dot(a_ref[...], b_ref[...], preferred_element_type=F32)


def _mm_res_acc_kernel(a_ref, b_ref, r_ref, o_ref):
    @pl.when(pl.program_id(2) == 0)
    def _():
        o_ref[...] = r_ref[...]

    o_ref[...] += jnp.dot(a_ref[...], b_ref[...], preferred_element_type=F32)


def matmul(a, b, *, out_dtype=F32, residual=None, tm=512, tn=1024):
    m, kd = a.shape
    n = b.shape[1]
    tm, tn = min(tm, m), min(tn, n)
    in_specs = [pl.BlockSpec((tm, kd), lambda j, i: (i, 0)),
                pl.BlockSpec((kd, tn), lambda j, i: (0, j))]
    args = [a, b]
    body = _mm_kernel
    if residual is not None:
        body = _mm_res_kernel
        in_specs.append(pl.BlockSpec((tm, tn), lambda j, i: (i, j)))
        args.append(residual)
    return pl.pallas_call(
        body,
        grid=(n // tn, m // tm),
        in_specs=in_specs,
        out_specs=pl.BlockSpec((tm, tn), lambda j, i: (i, j)),
        out_shape=jax.ShapeDtypeStruct((m, n), out_dtype),
        compiler_params=_params(2),
        name="matmul_res" if residual is not None else "matmul",
    )(*args)


def _mm_parts_res_kernel(*refs):
    *a_refs, b_ref, r_ref, o_ref = refs
    width = a_refs[0].shape[1]
    acc = r_ref[...]
    for g, a_ref in enumerate(a_refs):
        acc = acc + jnp.dot(a_ref[...], b_ref[g * width:(g + 1) * width, :], preferred_element_type=F32)
    o_ref[...] = acc


def matmul_parts_res(parts, b, residual, tm=512, tn=1024):
    m, w = parts[0].shape
    kd, n = b.shape
    return pl.pallas_call(
        _mm_parts_res_kernel,
        grid=(n // tn, m // tm),
        in_specs=[pl.BlockSpec((tm, w), lambda j, i: (i, 0)) for _ in parts] + [
            pl.BlockSpec((kd, tn), lambda j, i: (0, j)),
            pl.BlockSpec((tm, tn), lambda j, i: (i, j))],
        out_specs=pl.BlockSpec((tm, tn), lambda j, i: (i, j)),
        out_shape=jax.ShapeDtypeStruct((m, n), F32),
        compiler_params=_params(2),
        name="matmul_parts_res",
    )(*parts, b, residual)


def matmul_res_ktiled(a, b, residual, tm=1024, tn=1024, tk=2048):
    m, kd = a.shape
    n = b.shape[1]
    return pl.pallas_call(
        _mm_res_acc_kernel,
        grid=(m // tm, n // tn, kd // tk),
        in_specs=[pl.BlockSpec((tm, tk), lambda i, j, k: (i, k)),
                  pl.BlockSpec((tk, tn), lambda i, j, k: (k, j)),
                  pl.BlockSpec((tm, tn), lambda i, j, k: (i, j))],
        out_specs=pl.BlockSpec((tm, tn), lambda i, j, k: (i, j)),
        out_shape=jax.ShapeDtypeStruct((m, n), F32),
        compiler_params=_params(3),
        name="matmul_res_ktiled",
    )(a, b, residual)


def _gelu_exact(x):
    return 0.5 * x * (1.0 + lax.erf(x * (1.0 / math.sqrt(2.0))))


def _peer_act_kernel(h_ref, u_ref, g_ref, o_ref):
    tm, n_key0, n_key1 = g_ref.shape
    g = jnp.swapaxes(g_ref[...].reshape(tm // SUBLANES, SUBLANES, n_key0, n_key1), 1, 2)
    act = _gelu_exact(lax.dot_general(h_ref[...], u_ref[...], _NT, preferred_element_type=F32))
    for a in range(n_key0):
        cols = slice(a * n_key1, (a + 1) * n_key1)
        o_ref[:, cols] = (act[:, cols] * g[:, a].reshape(tm, n_key1)).astype(o_ref.dtype)


def peer_gated_act(hn, u, gates, tm=1024, tn=1024):
    t, d = hn.shape
    e = u.shape[0]
    tm = min(tm, t)
    n_key0 = tn // PEER_NKEYS
    return pl.pallas_call(
        _peer_act_kernel,
        grid=(e // tn, t // tm),
        in_specs=[pl.BlockSpec((tm, d), lambda j, i: (i, 0)),
                  pl.BlockSpec((tn, d), lambda j, i: (j, 0)),
                  pl.BlockSpec((tm, n_key0, PEER_NKEYS), lambda j, i: (i, j, 0))],
        out_specs=pl.BlockSpec((tm, tn), lambda j, i: (i, j)),
        out_shape=jax.ShapeDtypeStruct((t, e), BF16),
        compiler_params=_params(2),
        name="peer_gated_act",
    )(hn, u, gates)


def _rms_lanes(x, g):
    ms = jnp.mean(x * x, axis=-1, keepdims=True)
    return x * lax.rsqrt(ms + EPS) * g


def _rms_halves(x, g):
    lane = lax.broadcasted_iota(I32, x.shape, 1)
    lo = lane < DIFF_DH
    xx = x * x
    ms_lo = jnp.sum(jnp.where(lo, xx, 0.0), axis=-1, keepdims=True) * (1.0 / DIFF_DH)
    ms_hi = jnp.sum(jnp.where(lo, 0.0, xx), axis=-1, keepdims=True) * (1.0 / DIFF_DH)
    inv = jnp.where(lo, lax.rsqrt(ms_lo + EPS), lax.rsqrt(ms_hi + EPS))
    return x * inv * g


def _exp2_weights(s2):
    return jnp.exp2(s2 - jnp.max(s2, axis=-1, keepdims=True))


def _values_with_ones(v):
    ones_col = jnp.where(lax.broadcasted_iota(I32, v.shape, 1) == 0, 1.0, 0.0)
    return jnp.concatenate([v, ones_col], axis=-1).astype(BF16)


def _weighted_values(p, vx):
    ov = jnp.dot(p.astype(BF16), vx, preferred_element_type=F32)
    return ov[:, :HEAD_DIM], ov[:, HEAD_DIM:HEAD_DIM + 1]


def _alibi_tables_kernel(slope_ref, alibi_ref, dil_ref, mult_ref, *, s_len):
    shape = alibi_ref.shape
    d = lax.broadcasted_iota(I32, shape, 1) - lax.broadcasted_iota(I32, shape, 0) - (s_len - TAB_ROWS)
    ad = jnp.abs(d)
    alibi = (-LOG2E * slope_ref[pl.program_id(0)]) * ad.astype(F32)
    mult = jnp.zeros(shape, F32)
    for dil in DIL_DILATIONS:
        mult = mult + jnp.where(((d & (dil - 1)) == 0) & (ad <= DIL_HALF * dil), 1.0, 0.0)
    alibi_ref[...] = alibi
    dil_ref[...] = jnp.where(mult > 0.0, alibi, NEG_INF)
    mult_ref[...] = mult


def alibi_tables(s_len):
    slopes = jnp.asarray(2.0 ** (-8.0 * np.arange(1, N_HEADS + 1) / N_HEADS), F32)
    width = 2 * s_len - TAB_ROWS
    per_head = pl.BlockSpec((None, TAB_ROWS, width), lambda h: (h, 0, 0))
    return pl.pallas_call(
        functools.partial(_alibi_tables_kernel, s_len=s_len),
        grid=(N_HEADS,),
        in_specs=[pl.BlockSpec(memory_space=pltpu.SMEM)],
        out_specs=[per_head, per_head, pl.BlockSpec((TAB_ROWS, width), lambda h: (0, 0))],
        out_shape=[jax.ShapeDtypeStruct((N_HEADS, TAB_ROWS, width), F32),
                   jax.ShapeDtypeStruct((N_HEADS, TAB_ROWS, width), F32),
                   jax.ShapeDtypeStruct((TAB_ROWS, width), F32)],
        compiler_params=_params(1),
        name="alibi_tables",
    )(slopes)


def _table_tile(tab_ref, r0, k_lo, k_hi, s_len):
    row = r0 % TAB_ROWS
    lane0 = s_len - TAB_ROWS - (r0 - row) + k_lo
    return tab_ref[row:row + ATT_ROWS, lane0:lane0 + (k_hi - k_lo)]


def _head_specs(s, col0, width=HEAD_DIM):
    return [pl.BlockSpec((None, s, width), lambda h, bi, c=c: (bi, 0, col0 + c * N_HEADS + h)) for c in range(3)]


def _diff_attn_kernel(q_ref, k_ref, v_ref, gq_ref, gk_ref, go_ref, lam_ref, tab_ref, o_ref, kn_ref, vb_ref,
                      *, lambda_init):
    s_len = k_ref.shape[0]
    kn_ref[...] = _rms_halves(k_ref[...], gk_ref[...]).astype(BF16)
    vb_ref[...] = _values_with_ones(v_ref[...])
    lv = lam_ref[...]
    lam = (jnp.exp(jnp.sum(lv[0:1] * lv[1:2], axis=-1, keepdims=True))
           - jnp.exp(jnp.sum(lv[2:3] * lv[3:4], axis=-1, keepdims=True)) + lambda_init)
    lo = lax.broadcasted_iota(I32, (ATT_ROWS, HEAD_DIM), 1) < DIFF_DH
    def scores(r0):
        qn = _rms_halves(q_ref[r0:r0 + ATT_ROWS, :], gq_ref[...]) * (DIFF_DH ** -0.5 * LOG2E)
        tab = _table_tile(tab_ref, r0, 0, s_len, s_len)
        return tuple(lax.dot_general(qc.astype(BF16), kn_ref[...], _NT, preferred_element_type=F32) + tab
                     for qc in (jnp.where(lo, qn, 0.0), jnp.where(lo, 0.0, qn)))

    s2_next = scores(0)
    for r0 in range(0, s_len, ATT_ROWS):
        s2 = s2_next
        if r0 + ATT_ROWS < s_len:
            s2_next = scores(r0 + ATT_ROWS)
        o0, l0 = _weighted_values(_exp2_weights(s2[0]), vb_ref[...])
        o1, l1 = _weighted_values(_exp2_weights(s2[1]), vb_ref[...])
        o = o0 * (1.0 / l0) - o1 * (lam / l1)
        o_ref[r0:r0 + ATT_ROWS, :] = (_rms_lanes(o, go_ref[...]) * (1.0 - lambda_init)).astype(o_ref.dtype)


def diff_attention(p, col0, lam_vecs, g_qk, g_out, lambda_init, alibi):
    b, s, _ = p.shape
    lam_pad = jnp.pad(lam_vecs, ((0, 0), (0, HEAD_DIM - DIFF_DH)))
    gq = jnp.tile(g_qk[0], 2).reshape(1, HEAD_DIM)
    gk = jnp.tile(g_qk[1], 2).reshape(1, HEAD_DIM)
    vec = pl.BlockSpec((1, HEAD_DIM), lambda h, bi: (0, 0))
    return pl.pallas_call(
        functools.partial(_diff_attn_kernel, lambda_init=lambda_init),
        grid=(N_HEADS, b),
        in_specs=_head_specs(s, col0) + [
            vec, vec, vec,
            pl.BlockSpec((4, HEAD_DIM), lambda h, bi: (0, 0)),
            pl.BlockSpec((None,) + alibi.shape[1:], lambda h, bi: (h, 0, 0))],
        out_specs=pl.BlockSpec((None, s, HEAD_DIM), lambda h, bi: (bi, 0, h)),
        out_shape=jax.ShapeDtypeStruct((b, s, GROUP_WIDTH), BF16),
        scratch_shapes=[pltpu.VMEM((s, HEAD_DIM), BF16), pltpu.VMEM((s, 2 * HEAD_DIM), BF16)],
        compiler_params=_params(2),
        name="diff_attention",
    )(p, p, p, gq, gk, g_out.reshape(1, HEAD_DIM), lam_pad, alibi)


def _dilated_attn_kernel(q_ref, k_ref, v_ref, gq_ref, gk_ref, tab_ref, mult_ref, o_ref, kn_ref, vb_ref):
    s_len = k_ref.shape[0]
    kn_ref[...] = _rms_lanes(k_ref[...], gk_ref[...]).astype(BF16)
    vb_ref[...] = _values_with_ones(v_ref[...])
    def key_window(r0):
        return max(0, r0 - DIL_REACH), min(s_len, r0 + ATT_ROWS + DIL_REACH)

    def scores(r0):
        k_lo, k_hi = key_window(r0)
        qn = (_rms_lanes(q_ref[r0:r0 + ATT_ROWS, :], gq_ref[...]) * (HEAD_DIM ** -0.5 * LOG2E)).astype(BF16)
        return (lax.dot_general(qn, kn_ref[k_lo:k_hi, :], _NT, preferred_element_type=F32)
                + _table_tile(tab_ref, r0, k_lo, k_hi, s_len))

    s2_next = scores(0)
    for r0 in range(0, s_len, ATT_ROWS):
        k_lo, k_hi = key_window(r0)
        s2 = s2_next
        if r0 + ATT_ROWS < s_len:
            s2_next = scores(r0 + ATT_ROWS)
        o, l = _weighted_values(_exp2_weights(s2) * _table_tile(mult_ref, r0, k_lo, k_hi, s_len),
                                vb_ref[k_lo:k_hi, :])
        o_ref[r0:r0 + ATT_ROWS, :] = (o * (1.0 / l)).astype(o_ref.dtype)


def dilated_attention(p, col0, g_qk, dil_tab, mult_tab):
    b, s, _ = p.shape
    vec = pl.BlockSpec((1, HEAD_DIM), lambda h, bi: (0, 0))
    return pl.pallas_call(
        _dilated_attn_kernel,
        grid=(N_HEADS, b),
        in_specs=_head_specs(s, col0) + [
            vec, vec,
            pl.BlockSpec((None,) + dil_tab.shape[1:], lambda h, bi: (h, 0, 0)),
            pl.BlockSpec(mult_tab.shape, lambda h, bi: (0, 0))],
        out_specs=pl.BlockSpec((None, s, HEAD_DIM), lambda h, bi: (bi, 0, h)),
        out_shape=jax.ShapeDtypeStruct((b, s, GROUP_WIDTH), BF16),
        scratch_shapes=[pltpu.VMEM((s, HEAD_DIM), BF16), pltpu.VMEM((s, 2 * HEAD_DIM), BF16)],
        compiler_params=_params(2),
        name="dilated_attention",
    )(p, p, p, g_qk[0].reshape(1, HEAD_DIM), g_qk[1].reshape(1, HEAD_DIM), dil_tab, mult_tab)


def _na_window_start(q_row0, n_rows):
    return min(max(q_row0 - NA_ROWS // 2, 0), n_rows - NA_K_ROWS)


def _na_attn_kernel(q_ref, k_ref, v_ref, gq_ref, gk_ref, bias_ref, o_ref, kn_ref, vb_ref):
    n_rows = k_ref.shape[0] // GRID_W
    kn_ref[...] = _rms_lanes(k_ref[...], gk_ref[...]).astype(BF16)
    vb_ref[...] = _values_with_ones(v_ref[...])
    tq, tk = NA_Q_ROWS * GRID_W, NA_K_ROWS * GRID_W
    n_blocks = n_rows // NA_Q_ROWS

    def scores(qb):
        k0 = _na_window_start(qb * NA_Q_ROWS, n_rows) * GRID_W
        qn = (_rms_lanes(q_ref[qb * tq:(qb + 1) * tq, :], gq_ref[...]) * (HEAD_DIM ** -0.5 * LOG2E)).astype(BF16)
        return lax.dot_general(qn, kn_ref[k0:k0 + tk, :], _NT, preferred_element_type=F32) + bias_ref[qb]

    s2_next = scores(0)
    for qb in range(n_blocks):
        k0 = _na_window_start(qb * NA_Q_ROWS, n_rows) * GRID_W
        s2 = s2_next
        if qb + 1 < n_blocks:
            s2_next = scores(qb + 1)
        o, l = _weighted_values(_exp2_weights(s2), vb_ref[k0:k0 + tk, :])
        o_ref[qb * tq:(qb + 1) * tq, :] = (o * (1.0 / l)).astype(o_ref.dtype)


def _na_bias_kernel(t_ref, o_ref, *, n_rows):
    lane = lax.broadcasted_iota(I32, (GRID_W, 2 * GRID_W), 1)
    for qb in range(n_rows // NA_Q_ROWS):
        q_row0 = qb * NA_Q_ROWS
        k_row0 = _na_window_start(q_row0, n_rows)
        for a in range(NA_Q_ROWS):
            rq = q_row0 + a
            rs = min(max(rq - NA_ROWS // 2, 0), n_rows - NA_ROWS)
            for wp in range(NA_K_ROWS // 2):
                rk = k_row0 + 2 * wp
                ok0, ok1 = rs <= rk < rs + NA_ROWS, rs <= rk + 1 < rs + NA_ROWS
                m = min(max(rk - rq + NA_ROWS, 0), 2 * NA_ROWS - 1)
                tile = t_ref[m] * LOG2E
                if not (ok0 and ok1):
                    keep = (lane < GRID_W) if ok0 else (lane >= GRID_W)
                    tile = jnp.where(keep, tile, NEG_INF) if (ok0 or ok1) else jnp.full_like(tile, NEG_INF)
                o_ref[qb, a * GRID_W:(a + 1) * GRID_W, wp * 2 * GRID_W:(wp + 1) * 2 * GRID_W] = tile


def _na_bias(rpb, n_rows):
    n_rel_r = 2 * NA_ROWS - 1
    pad = GRID_W - NA_COLS
    rp = jnp.pad(rpb, ((0, 0), (0, 0), (pad, pad)))
    toe = jnp.stack([rp[:, :, NA_COLS - 1 + pad - cq:NA_COLS - 1 + pad - cq + GRID_W] for cq in range(GRID_W)],
                    axis=2)
    cq = np.arange(GRID_W)
    cs = np.clip(cq - NA_COLS // 2, 0, GRID_W - NA_COLS)
    col_ok = (cq[None, :] >= cs[:, None]) & (cq[None, :] < cs[:, None] + NA_COLS)
    toe = jnp.where(col_ok[None, None], toe, NEG_INF)
    ext = jnp.pad(toe, ((0, 0), (1, 1), (0, 0), (0, 0)), constant_values=NEG_INF)
    pairs = jnp.concatenate([ext[:, :n_rel_r + 1], ext[:, 1:]], axis=-1)
    tq, tk = NA_Q_ROWS * GRID_W, NA_K_ROWS * GRID_W
    nqb = n_rows // NA_Q_ROWS
    return pl.pallas_call(
        functools.partial(_na_bias_kernel, n_rows=n_rows),
        grid=(N_HEADS,),
        in_specs=[pl.BlockSpec((None, n_rel_r + 1, GRID_W, 2 * GRID_W), lambda h: (h, 0, 0, 0))],
        out_specs=pl.BlockSpec((None, nqb, tq, tk), lambda h: (h, 0, 0, 0)),
        out_shape=jax.ShapeDtypeStruct((N_HEADS, nqb, tq, tk), F32),
        compiler_params=_params(1),
        name="na_bias",
    )(pairs)


def neighbourhood_attention(p, col0, g_qk, rpb):
    b, s, _ = p.shape
    bias = _na_bias(rpb, s // GRID_W)
    vec = pl.BlockSpec((1, HEAD_DIM), lambda h, bi: (0, 0))
    return pl.pallas_call(
        _na_attn_kernel,
        grid=(N_HEADS, b),
        in_specs=_head_specs(s, col0) + [
            vec, vec,
            pl.BlockSpec((None,) + bias.shape[1:], lambda h, bi: (h, 0, 0, 0))],
        out_specs=pl.BlockSpec((None, s, HEAD_DIM), lambda h, bi: (bi, 0, h)),
        out_shape=jax.ShapeDtypeStruct((b, s, GROUP_WIDTH), BF16),
        scratch_shapes=[pltpu.VMEM((s, HEAD_DIM), BF16), pltpu.VMEM((s, 2 * HEAD_DIM), BF16)],
        compiler_params=_params(2),
        name="neighbourhood_attention",
    )(p, p, p, g_qk[0].reshape(1, HEAD_DIM), g_qk[1].reshape(1, HEAD_DIM), bias)


def _mla_norm_rope(x, g, cos, sin):
    ms = jnp.sum(x * x, axis=-1, keepdims=True) * (1.0 / MLA_QK)
    xn = x * lax.rsqrt(ms + EPS) * g
    xr = xn[:, HEAD_DIM:]
    partner = pltpu.roll(xr, QK_ROPE // 2, 1) + pltpu.roll(xr, HEAD_DIM - QK_ROPE // 2, 1)
    return jnp.concatenate([xn[:, :HEAD_DIM], xr * cos + partner * sin], axis=-1)


def _mla_attn_kernel(q_ref, kn_in_ref, v_ref, kr_ref, cos_ref, sin_ref, gq_ref, gk_ref, o_ref, kf_ref, vb_ref):
    s_len = v_ref.shape[0]
    k = jnp.concatenate([kn_in_ref[...], kr_ref[...]], axis=-1)
    kf_ref[...] = _mla_norm_rope(k, gk_ref[...], cos_ref[...], sin_ref[...]).astype(BF16)
    vb_ref[...] = _values_with_ones(v_ref[...])
    def scores(r0):
        rows = slice(r0, r0 + ATT_ROWS)
        qf = (_mla_norm_rope(q_ref[rows, :], gq_ref[...], cos_ref[rows, :], sin_ref[rows, :])
              * (MLA_QK ** -0.5 * LOG2E)).astype(BF16)
        return lax.dot_general(qf, kf_ref[...], _NT, preferred_element_type=F32)

    s2_next = scores(0)
    for r0 in range(0, s_len, ATT_ROWS):
        s2 = s2_next
        if r0 + ATT_ROWS < s_len:
            s2_next = scores(r0 + ATT_ROWS)
        o, l = _weighted_values(_exp2_weights(s2), vb_ref[...])
        o_ref[r0:r0 + ATT_ROWS, :] = (o * (1.0 / l)).astype(o_ref.dtype)


def mla_attention(q_up, kv_up, k_rope, g_qk, cos_t, sin_t):
    b, s, _ = q_up.shape
    pad = MLA_QK_PAD - MLA_QK
    gq = jnp.pad(g_qk[0], (0, pad)).reshape(1, MLA_QK_PAD)
    gk = jnp.pad(g_qk[1], (0, pad)).reshape(1, MLA_QK_PAD)
    vec = pl.BlockSpec((1, MLA_QK_PAD), lambda bi, h: (0, 0))
    rope = pl.BlockSpec((s, HEAD_DIM), lambda bi, h: (0, 0))
    return pl.pallas_call(
        _mla_attn_kernel,
        grid=(b, N_HEADS),
        in_specs=[pl.BlockSpec((None, s, MLA_QK_PAD), lambda bi, h: (bi, 0, h)),
                  pl.BlockSpec((None, s, HEAD_DIM), lambda bi, h: (bi, 0, 2 * h)),
                  pl.BlockSpec((None, s, HEAD_DIM), lambda bi, h: (bi, 0, 2 * h + 1)),
                  pl.BlockSpec((None, s, HEAD_DIM), lambda bi, h: (bi, 0, 0)),
                  rope, rope, vec, vec],
        out_specs=pl.BlockSpec((None, s, HEAD_DIM), lambda bi, h: (bi, 0, h)),
        out_shape=jax.ShapeDtypeStruct((b, s, GROUP_WIDTH), BF16),
        scratch_shapes=[pltpu.VMEM((s, MLA_QK_PAD), BF16), pltpu.VMEM((s, 2 * HEAD_DIM), BF16)],
        compiler_params=_params(2),
        name="mla_attention",
    )(q_up, kv_up, kv_up, k_rope, cos_t, sin_t, gq, gk)


def _top_k_lead(s, order, payload=None):
    vals, picks = [], []
    for _ in range(PEER_TOPK):
        m = jnp.max(s, axis=0, keepdims=True)
        first = jnp.min(jnp.where(s == m, order, jnp.iinfo(jnp.int32).max), axis=0, keepdims=True)
        sel = order == first
        vals.append(m)
        picks.append(first if payload is None else jnp.max(jnp.where(sel, payload, 0), axis=0, keepdims=True))
        s = jnp.where(sel, -jnp.inf, s)
    return jnp.concatenate(vals, axis=0), jnp.concatenate(picks, axis=0)


_PAIRS = tuple((i, j) for i in range(PEER_TOPK) for j in range(PEER_TOPK) if (i + 1) * (j + 1) <= PEER_TOPK)


def _peer_route_kernel(q_ref, keys_ref, expert_ref, gate_ref):
    groups = q_ref.shape[0] // HEAD_DIM
    tile = (1, groups, HEAD_DIM)
    key_id = lax.broadcasted_iota(I32, (PEER_NKEYS,) + tile[1:], 0)
    cand_flat = jnp.concatenate([jnp.full(tile, i * PEER_TOPK + j, I32) for i, j in _PAIRS], axis=0)
    tops = []
    for hc in range(keys_ref.shape[0]):
        cols = slice(hc * HEAD_DIM, (hc + 1) * HEAD_DIM)
        sc = jnp.stack([lax.dot_general(keys_ref[hc], q_ref[g * HEAD_DIM:(g + 1) * HEAD_DIM, cols].astype(BF16),
                                        _NT, preferred_element_type=F32) for g in range(groups)], axis=0)
        tops.append(_top_k_lead(jnp.swapaxes(sc, 0, 1), key_id))
    for hh in range(keys_ref.shape[0] // 2):
        (s0, i0), (s1, i1) = tops[2 * hh], tops[2 * hh + 1]
        cand_s = jnp.concatenate([s0[i:i + 1] + s1[j:j + 1] for i, j in _PAIRS], axis=0)
        cand_e = jnp.concatenate([i0[i:i + 1] * PEER_NKEYS + i1[j:j + 1] for i, j in _PAIRS], axis=0)
        best_s, best_e = _top_k_lead(cand_s, cand_flat, cand_e)
        e = jnp.exp(best_s - jnp.max(best_s, axis=0, keepdims=True))
        gate = jnp.swapaxes(e * (1.0 / jnp.sum(e, axis=0, keepdims=True)), 0, 1)
        expert = jnp.swapaxes(best_e, 0, 1)
        rows = slice(hh * PEER_TOPK, (hh + 1) * PEER_TOPK)
        for g in range(groups):
            gate_ref[rows, g * HEAD_DIM:(g + 1) * HEAD_DIM] = gate[g]
            expert_ref[rows, g * HEAD_DIM:(g + 1) * HEAD_DIM] = expert[g]


def peer_route(q, sub_keys, tt=1024, heads_per_step=2):
    t = q.shape[0]
    n_sel = PEER_HEADS * PEER_TOPK
    hs = heads_per_step
    return pl.pallas_call(
        _peer_route_kernel,
        grid=(t // tt, PEER_HEADS // hs),
        in_specs=[pl.BlockSpec((tt, 2 * hs * HEAD_DIM), lambda i, h: (i, h)),
                  pl.BlockSpec((2 * hs, PEER_NKEYS, HEAD_DIM), lambda i, h: (h, 0, 0))],
        out_specs=[pl.BlockSpec((hs * PEER_TOPK, tt), lambda i, h: (h, i)),
                   pl.BlockSpec((hs * PEER_TOPK, tt), lambda i, h: (h, i))],
        out_shape=[jax.ShapeDtypeStruct((n_sel, t), I32), jax.ShapeDtypeStruct((n_sel, t), F32)],
        compiler_params=_params(2),
        name="peer_route",
    )(q, sub_keys)


def _peer_gate_matrix_kernel(expert_ref, gate_ref, o_ref, e_s, g_s):
    e_s[...] = expert_ref[...].T
    g_s[...] = gate_ref[...].T
    n_sel = e_s.shape[1]
    row = lax.broadcasted_iota(I32, (PEER_NKEYS, n_sel), 0)

    for t in range(e_s.shape[0]):
        e_row = e_s[t:t + 1, :]
        g_row = g_s[t:t + 1, :]
        hi = jnp.where(row == (e_row >> 7), g_row, 0.0).astype(BF16)
        lo = jnp.where(row == (e_row & (PEER_NKEYS - 1)), 1.0, 0.0).astype(BF16)
        o_ref[t] = lax.dot_general(hi, lo, _NT, preferred_element_type=F32)


def peer_gate_matrix(expert_t, gate_t, tg=128):
    n_sel, t = expert_t.shape
    return pl.pallas_call(
        _peer_gate_matrix_kernel,
        grid=(t // tg,),
        in_specs=[pl.BlockSpec((n_sel, tg), lambda i: (0, i)), pl.BlockSpec((n_sel, tg), lambda i: (0, i))],
        out_specs=pl.BlockSpec((tg, PEER_NKEYS, PEER_NKEYS), lambda i: (i, 0, 0)),
        out_shape=jax.ShapeDtypeStruct((t, PEER_NKEYS, PEER_NKEYS), F32),
        scratch_shapes=[pltpu.VMEM((tg, n_sel), I32), pltpu.VMEM((tg, n_sel), F32)],
        compiler_params=_params(1),
        name="peer_gate_matrix",
    )(expert_t, gate_t)


def _rope_tables(s):
    inv_freq = 1.0 / (ROPE_THETA ** (np.arange(0, QK_ROPE, 2, dtype=np.float32) / QK_ROPE))
    ang = jnp.arange(s, dtype=F32)[:, None] * jnp.asarray(inv_freq, F32)[None, :]
    cos, sin = jnp.cos(ang), jnp.sin(ang)
    zeros = jnp.zeros((s, HEAD_DIM - QK_ROPE), F32)
    return (jnp.concatenate([cos, cos, zeros], axis=-1), jnp.concatenate([-sin, sin, zeros], axis=-1))


def _layer(x2, b, s, l, tables, g_mix, w_in, w_out, diff_lambda, diff_g_qk, diff_g_out, dil_g_qk, na_g_qk,
           na_rpb, mla_g_cq, mla_g_ckv, mla_w_uq, mla_w_ukv, mla_g_qk, g_ffn, peer_w_q, peer_sub_keys, peer_u,
           peer_v):
    t = b * s
    cos_t, sin_t, alibi, dil_tab, mult_tab = tables
    lambda_init = 0.8 - 0.6 * math.exp(-0.3 * l)
    n_qkv = 3 * QKV_COLS

    h = rms_rows(x2, g_mix)
    p = matmul(h, w_in[:, :n_qkv].astype(BF16), tm=1024).reshape(b, s, n_qkv)
    w_lat = jnp.pad(w_in[:, n_qkv:], ((0, 0), (0, HEAD_DIM - QK_ROPE))).astype(BF16)
    p_lat = matmul(h, w_lat, tn=MLA_LATENT_PAD)
    c_q, c_kv, k_rope = mla_latent_norm(p_lat, mla_g_cq, mla_g_ckv)

    blocks = QKV_COLS // HEAD_DIM
    o_a = diff_attention(p, 0, diff_lambda, diff_g_qk, diff_g_out, lambda_init, alibi)
    o_b = dilated_attention(p, blocks, dil_g_qk, dil_tab, mult_tab)
    o_c = neighbourhood_attention(p, 2 * blocks, na_g_qk, na_rpb)

    w_uq = jnp.pad(mla_w_uq.reshape(Q_LORA, N_HEADS, MLA_QK), ((0, 0), (0, 0), (0, MLA_QK_PAD - MLA_QK)))
    q_up = matmul(c_q, w_uq.reshape(Q_LORA, N_HEADS * MLA_QK_PAD).astype(BF16), tm=1024)
    kv_up = matmul(c_kv, mla_w_ukv.astype(BF16), tm=1024)
    o_d = mla_attention(q_up.reshape(b, s, -1), kv_up.reshape(b, s, -1), k_rope.reshape(b, s, HEAD_DIM),
                        mla_g_qk, cos_t, sin_t)

    mixers = [o.reshape(t, GROUP_WIDTH) for o in (o_a, o_b, o_c, o_d)]
    x2 = matmul_parts_res(mixers, w_out.astype(BF16), x2)

    hf = rms_rows(x2, g_ffn)
    q = matmul(hf, peer_w_q.astype(BF16), tm=1024)
    keys = peer_sub_keys.reshape(2 * PEER_HEADS, PEER_NKEYS, HEAD_DIM).astype(BF16)
    expert_t, gate_t = peer_route(q, keys)
    gates = peer_gate_matrix(expert_t, gate_t)
    act = peer_gated_act(hf, peer_u.astype(BF16), gates)
    return matmul_res_ktiled(act, peer_v.astype(BF16), x2)


def kernel(x, g_mix, w_in, w_out, diff_lambda, diff_g_qk, diff_g_out, dil_g_qk, na_g_qk, na_rpb, mla_g_cq,
           mla_g_ckv, mla_w_uq, mla_w_ukv, mla_g_qk, g_ffn, peer_w_q, peer_sub_keys, peer_u, peer_v):
    b, s, d = x.shape
    depth = g_mix.shape[0]
    tables = _rope_tables(s) + tuple(alibi_tables(s))
    x2 = x.reshape(b * s, d)
    per_layer = (g_mix, w_in, w_out, diff_lambda, diff_g_qk, diff_g_out, dil_g_qk, na_g_qk, na_rpb, mla_g_cq,
                 mla_g_ckv, mla_w_uq, mla_w_ukv, mla_g_qk, g_ffn, peer_w_q, peer_sub_keys, peer_u, peer_v)
    for l in range(depth):
        x2 = _layer(x2, b, s, l, tables, *(w[l] for w in per_layer))
    return x2.reshape(b, s, d)
```
